```python
import math
import jax, jax.numpy as jnp
from jax import lax
import numpy as np

D_MODEL = 4096
BATCH = 4
SEQ = 4096
DEPTH = 2

GDN_HEADS = 16
GDN_DK = 128
GDN_DV = 128
GDN_CONV = 4
GDN_CHUNK = 64
NSA_HEADS = 16
NSA_KV_GROUPS = 4
NSA_DK = 128
NSA_DV = 128
CMP_BLOCK = 32
CMP_STRIDE = 16
SLC_BLOCK = 64
SLC_TOPK = 16
WINDOW = 512
NSA_Q_BLOCK = 64
D_FF = 4 * D_MODEL
N_ADA = 6
DN_ALPHA = (2.0 * DEPTH) ** 0.25
DN_BETA = (8.0 * DEPTH) ** -0.25
LN_EPS = 1e-5
RMS_EPS = 1e-6
NEG_INF = -1e30
FORCED_SCORE = 1e6

GDN_QK = GDN_HEADS * GDN_DK
GDN_VW = GDN_HEADS * GDN_DV
NSA_QW = NSA_HEADS * NSA_DK
NSA_KW = NSA_KV_GROUPS * NSA_DK
NSA_VW = NSA_KV_GROUPS * NSA_DV
NSA_OW = NSA_HEADS * NSA_DV

IN_SIZES = (GDN_QK, GDN_QK, GDN_VW, GDN_VW, GDN_HEADS, GDN_HEADS,
            NSA_QW, NSA_KW, NSA_VW, NSA_KW, NSA_VW, NSA_KW, NSA_VW, 3 * NSA_HEADS,
            D_MODEL, D_MODEL)
IN_IS_VALUE = (False, False, True, False, False, False,
               False, False, True, False, True, False, True, False,
               False, False)
N_IN = sum(IN_SIZES)

kernel_name = 'hybrid_gdn_nsa_deepnorm_adaln_block'


def _split_points():
    pts, acc = [], 0
    for n in IN_SIZES[:-1]:
        acc += n
        pts.append(acc)
    return pts


def layer_norm(x, g, b):
    xf = x.astype(jnp.float32)
    mu = jnp.mean(xf, axis=-1, keepdims=True)
    var = jnp.mean(jnp.square(xf - mu), axis=-1, keepdims=True)
    return ((xf - mu) * lax.rsqrt(var + LN_EPS) * g.astype(jnp.float32) + b.astype(jnp.float32)).astype(x.dtype)


def l2_normalize(t):
    return t * lax.rsqrt(jnp.sum(jnp.square(t), axis=-1, keepdims=True) + RMS_EPS)


def alibi_slopes(n_heads):
    return jnp.exp2(-8.0 * jnp.arange(1, n_heads + 1, dtype=jnp.float32) / n_heads)


def causal_depthwise_conv(t, w):
    ch = t.shape[-1]
    return lax.conv_general_dilated(t, w[:, None, :].astype(t.dtype), window_strides=(1,),
                                    padding=[(w.shape[0] - 1, 0)],
                                    dimension_numbers=('NWC', 'WIO', 'NWC'),
                                    feature_group_count=ch)


def gated_delta_rule(q, k, v, beta, log_a):
    B, S, H, DK = q.shape
    DV = v.shape[-1]
    C = GDN_CHUNK
    N = S // C

    def to_chunks(t):
        return jnp.swapaxes(t.reshape(B, N, C, H, *t.shape[3:]), 2, 3)

    q, k, v, beta, log_a = (to_chunks(t) for t in (q, k, v, beta, log_a))
    g = jnp.cumsum(log_a, axis=-1)
    ti = jnp.arange(C)
    incl = ti[:, None] >= ti[None, :]
    strict = ti[:, None] > ti[None, :]
    diff = g[..., :, None] - g[..., None, :]
    decay = jnp.where(incl, jnp.exp(jnp.where(incl, diff, 0.0)), 0.0)
    kk = jnp.einsum('bnhtd,bnhsd->bnhts', k, k)
    a_mat = jnp.where(strict, beta[..., :, None] * kk * decay, 0.0) + jnp.eye(C, dtype=q.dtype)
    w = lax.linalg.triangular_solve(a_mat, (beta * jnp.exp(g))[..., None] * k,
                                    left_side=True, lower=True, unit_diagonal=True)
    u = lax.linalg.triangular_solve(a_mat, beta[..., None] * v,
                                    left_side=True, lower=True, unit_diagonal=True)
    qk = jnp.einsum('bnhtd,bnhsd->bnhts', q, k) * decay
    g_last = g[..., -1]
    k_end = k * jnp.exp(g_last[..., None] - g)[..., None]
    q_g = q * jnp.exp(g)[..., None]

    def step(state, xs):
        w_n, u_n, q_n, qk_n, k_n, gl_n = xs
        u_corr = u_n - jnp.einsum('bhcd,bhde->bhce', w_n, state)
        o = jnp.einsum('bhcd,bhde->bhce', q_n, state) + jnp.einsum('bhts,bhse->bhte', qk_n, u_corr)
        state = jnp.exp(gl_n)[..., None, None] * state + jnp.einsum('bhcd,bhce->bhde', k_n, u_corr)
        return state, o

    xs = tuple(jnp.swapaxes(t, 0, 1) for t in (w, u, q_g, qk, k_end, g_last))
    state0 = jnp.zeros((B, H, DK, DV), q.dtype)
    _, o = lax.scan(step, state0, xs)
    return jnp.swapaxes(jnp.swapaxes(o, 0, 1), 2, 3).reshape(B, S, H, DV)


def gated_deltanet_branch(pq, pk, pv, pz, pb, pa, conv_w, a_log, dt_bias, norm_w):
    B, S, _ = pq.shape
    f32 = jnp.float32
    qkv = jax.nn.silu(causal_depthwise_conv(jnp.concatenate([pq, pk, pv], axis=-1), conv_w)).astype(f32)
    q, k, v = jnp.split(qkv, [GDN_QK, 2 * GDN_QK], axis=-1)
    q = l2_normalize(q.reshape(B, S, GDN_HEADS, GDN_DK)) * GDN_DK ** -0.5
    k = l2_normalize(k.reshape(B, S, GDN_HEADS, GDN_DK))
    v = v.reshape(B, S, GDN_HEADS, GDN_DV)
    beta = jax.nn.sigmoid(pb.astype(f32))
    log_a = -jnp.exp(a_log.astype(f32)) * jax.nn.softplus(pa.astype(f32) + dt_bias.astype(f32))
    o = gated_delta_rule(q, k, v, beta, log_a)
    o = o * lax.rsqrt(jnp.mean(jnp.square(o), axis=-1, keepdims=True) + RMS_EPS) * norm_w.astype(f32)
    o = o * jax.nn.silu(pz.astype(f32).reshape(B, S, GDN_HEADS, GDN_DV))
    return o.reshape(B, S, GDN_VW).astype(pq.dtype)


def masked_softmax(s, valid):
    return jax.nn.softmax(jnp.where(valid, s, NEG_INF), axis=-1) * valid


def nsa_branch(nq, kc, vc, ksl, vsl, kw, vw, gate_logits, cmp_pos, cmp_w1, cmp_w2):
    B, S, _ = nq.shape
    G, HPG = NSA_KV_GROUPS, NSA_HEADS // NSA_KV_GROUPS
    f32 = jnp.float32
    q = nq.reshape(B, S, G, HPG, NSA_DK) * NSA_DK ** -0.5
    n_cmp = (S - CMP_BLOCK) // CMP_STRIDE + 1
    cmp_start = jnp.arange(n_cmp) * CMP_STRIDE
    cmp_end = cmp_start + CMP_BLOCK - 1
    gather_idx = cmp_start[:, None] + jnp.arange(CMP_BLOCK)[None, :]

    def compress(t, pos, w1, w2):
        blocks = t[:, gather_idx] + pos[:, None, :]
        hid = jax.nn.silu(jnp.einsum('bjrgd,rde->bjge', blocks, w1))
        return jnp.einsum('bjge,ef->bjgf', hid, w2)

    k_cmp = compress(kc.reshape(B, S, G, NSA_DK), cmp_pos[0], cmp_w1[0], cmp_w2[0])
    v_cmp = compress(vc.reshape(B, S, G, NSA_DV), cmp_pos[1], cmp_w1[1], cmp_w2[1])
    n_slc = S // SLC_BLOCK
    top_n = min(SLC_TOPK, n_slc)

    def to_blocks(t):
        return t.reshape(B, n_slc, SLC_BLOCK, G, -1).transpose(0, 3, 1, 2, 4).reshape(B, G, n_slc, -1)

    k_blk = to_blocks(ksl.reshape(B, S, G, NSA_DK))
    v_blk = to_blocks(vsl.reshape(B, S, G, NSA_DV))
    slc_start = jnp.arange(n_slc) * SLC_BLOCK
    cmp_to_slc = ((cmp_end[:, None] >= slc_start[None, :]) &
                  (cmp_start[:, None] <= slc_start[None, :] + SLC_BLOCK - 1)).astype(f32)
    pad = ((0, 0), (WINDOW, 0), (0, 0), (0, 0))
    k_win = jnp.pad(kw.reshape(B, S, G, NSA_DK), pad)
    v_win = jnp.pad(vw.reshape(B, S, G, NSA_DV), pad)
    gates = jax.nn.sigmoid(gate_logits.astype(f32)).reshape(B, S, 3, G, HPG)
    slopes = alibi_slopes(NSA_HEADS).reshape(G, HPG)
    TQ = NSA_Q_BLOCK

    def attend_block(i):
        t0 = i * TQ
        t = t0 + jnp.arange(TQ)
        qb = lax.dynamic_slice_in_dim(q, t0, TQ, axis=1)
        dist = (t[:, None] - cmp_end[None, :]).astype(f32)
        valid = (dist >= 0)[None, :, None, None, :]
        s = jnp.einsum('btghd,bjgd->btghj', qb, k_cmp, preferred_element_type=f32)
        s = s - slopes[:, :, None] * dist[:, None, None, :]
        p_cmp = masked_softmax(s, valid)
        o_cmp = jnp.einsum('btghj,bjgd->btghd', p_cmp.astype(v_cmp.dtype), v_cmp)
        imp = jnp.einsum('btghj,jn->btgn', p_cmp, cmp_to_slc)
        cur = (t // SLC_BLOCK)[:, None]
        blk = jnp.arange(n_slc)[None, :]
        allowed = (blk * SLC_BLOCK <= t[:, None])[None, :, None, :]
        forced = ((blk == 0) | (blk == cur) | (blk == cur - 1))[None, :, None, :]
        score = jnp.where(forced, FORCED_SCORE, jnp.where(allowed, imp, NEG_INF))
        top_val, top_idx = lax.top_k(score, top_n)
        flat_idx = top_idx.transpose(0, 2, 1, 3).reshape(B, G, TQ * top_n, 1)
        ks = jnp.take_along_axis(k_blk, flat_idx, axis=2).reshape(B, G, TQ, top_n, SLC_BLOCK, NSA_DK)
        vs = jnp.take_along_axis(v_blk, flat_idx, axis=2).reshape(B, G, TQ, top_n, SLC_BLOCK, NSA_DV)
        pos = top_idx[..., None] * SLC_BLOCK + jnp.arange(SLC_BLOCK)
        dist = (t[None, :, None, None, None] - pos).astype(f32)
        valid = ((top_val > 0.5 * NEG_INF)[..., None] & (dist >= 0))[:, :, :, None]
        s = jnp.einsum('btghd,bgtnld->btghnl', qb, ks, preferred_element_type=f32)
        s = s - slopes[None, None, :, :, None, None] * dist[:, :, :, None]
        p = masked_softmax(s.reshape(B, TQ, G, HPG, -1), valid.reshape(B, TQ, G, 1, -1)).reshape(s.shape)
        o_slc = jnp.einsum('btghnl,bgtnld->btghd', p.astype(vs.dtype), vs)
        ks_w = lax.dynamic_slice_in_dim(k_win, t0, TQ + WINDOW, axis=1)
        vs_w = lax.dynamic_slice_in_dim(v_win, t0, TQ + WINDOW, axis=1)
        pos = t0 - WINDOW + jnp.arange(TQ + WINDOW)
        dist = (t[:, None] - pos[None, :]).astype(f32)
        valid = ((pos[None, :] >= 0) & (dist >= 0) & (dist < WINDOW))[None, :, None, None, :]
        s = jnp.einsum('btghd,bsgd->btghs', qb, ks_w, preferred_element_type=f32)
        s = s - slopes[:, :, None] * dist[:, None, None, :]
        p = masked_softmax(s, valid)
        o_win = jnp.einsum('btghs,bsgd->btghd', p.astype(vs_w.dtype), vs_w)
        g = lax.dynamic_slice_in_dim(gates, t0, TQ, axis=1)[..., None]
        return (g[:, :, 0] * o_cmp + g[:, :, 1] * o_slc + g[:, :, 2] * o_win).astype(nq.dtype)

    out = lax.map(attend_block, jnp.arange(S // TQ))
    return out.transpose(1, 0, 2, 3, 4, 5).reshape(B, S, NSA_OW)


def hybrid_mixer(u, w_in, conv_w, a_log, dt_bias, norm_w, cmp_pos, cmp_w1, cmp_w2, w_read_a, w_read_b, w_out):
    proj = jnp.einsum('bsd,dn->bsn', u, w_in)
    (pq, pk, pv, pz, pb, pa, nq, kc, vc, ksl, vsl, kw, vw, ng, ma, mb) = jnp.split(proj, _split_points(), axis=-1)
    o_a = gated_deltanet_branch(pq, pk, pv, pz, pb, pa, conv_w, a_log, dt_bias, norm_w)
    o_b = nsa_branch(nq, kc, vc, ksl, vsl, kw, vw, ng, cmp_pos, cmp_w1, cmp_w2)
    y = (jax.nn.sigmoid(ma) * jnp.einsum('bsk,kd->bsd', o_a, w_read_a) +
         jax.nn.sigmoid(mb) * jnp.einsum('bsk,kd->bsd', o_b, w_read_b))
    return jnp.einsum('bsd,de->bse', y, w_out)


def setup_inputs(seed: int = 0) -> dict:
    key = jax.random.key(seed)
    ks = iter(jax.random.split(key, 32))
    f32 = jnp.float32
    L, D = DEPTH, D_MODEL

    def nrm(shape, scale):
        return jax.random.normal(next(ks), shape, f32) * scale

    col_scale = jnp.concatenate([jnp.full((n,), DN_BETA if is_v else 1.0, f32)
                                 for n, is_v in zip(IN_SIZES, IN_IS_VALUE)])
    dt = jnp.exp(jax.random.uniform(next(ks), (L, GDN_HEADS), f32, math.log(1e-3), math.log(1e-1)))
    a_init = jax.random.uniform(next(ks), (L, GDN_HEADS), f32, 1.0, 16.0)
    return {
        'x': nrm((BATCH, SEQ, D), 1.0),
        'c': nrm((BATCH, D), 1.0),
        'ada_w': nrm((D, N_ADA * D), 0.1 * D ** -0.5),
        'ada_b': nrm((N_ADA * D,), 0.02),
        'ada_table': nrm((L, N_ADA, D), 0.02),
        'w_in': nrm((L, D, N_IN), D ** -0.5) * col_scale,
        'gdn_conv_w': nrm((L, GDN_CONV, 2 * GDN_QK + GDN_VW), GDN_CONV ** -0.5),
        'gdn_a_log': jnp.log(a_init),
        'gdn_dt_bias': dt + jnp.log(-jnp.expm1(-dt)),
        'gdn_norm_w': 1.0 + nrm((L, GDN_DV), 0.02),
        'cmp_pos': nrm((L, 2, CMP_BLOCK, NSA_DK), 0.02),
        'cmp_w1': nrm((L, 2, CMP_BLOCK, NSA_DK, NSA_DK), (CMP_BLOCK * NSA_DK) ** -0.5),
        'cmp_w2': nrm((L, 2, NSA_DK, NSA_DK), NSA_DK ** -0.5),
        'w_read_a': nrm((L, GDN_VW, D), DN_BETA * GDN_VW ** -0.5),
        'w_read_b': nrm((L, NSA_OW, D), DN_BETA * NSA_OW ** -0.5),
        'w_out': nrm((L, D, D), DN_BETA * D ** -0.5),
        'ln1_g': 1.0 + nrm((L, D), 0.02),
        'ln1_b': nrm((L, D), 0.02),
        'mlp_w1': nrm((L, D, D_FF), DN_BETA * D ** -0.5),
        'mlp_b1': nrm((L, D_FF), 0.02),
        'mlp_w2': nrm((L, D_FF, D), DN_BETA * D_FF ** -0.5),
        'mlp_b2': nrm((L, D), 0.02),
        'ln2_g': 1.0 + nrm((L, D), 0.02),
        'ln2_b': nrm((L, D), 0.02),
    }


def reference(x, c, ada_w, ada_b, ada_table, w_in, gdn_conv_w, gdn_a_log, gdn_dt_bias, gdn_norm_w,
              cmp_pos, cmp_w1, cmp_w2, w_read_a, w_read_b, w_out, ln1_g, ln1_b,
              mlp_w1, mlp_b1, mlp_w2, mlp_b2, ln2_g, ln2_b):
    B, S, D = x.shape
    mod = jnp.einsum('bd,de->be', jax.nn.silu(c), ada_w) + ada_b
    for l in range(DEPTH):
        m = (mod + ada_table[l].reshape(-1)).reshape(B, N_ADA, 1, D)
        shift1, scale1, gate1, shift2, scale2, gate2 = (m[:, j] for j in range(N_ADA))
        u = x * (1 + scale1) + shift1
        h = hybrid_mixer(u, w_in[l], gdn_conv_w[l], gdn_a_log[l], gdn_dt_bias[l], gdn_norm_w[l],
                         cmp_pos[l], cmp_w1[l], cmp_w2[l], w_read_a[l], w_read_b[l], w_out[l])
        x = layer_norm(DN_ALPHA * x + (1 + gate1) * h, ln1_g[l], ln1_b[l])
        u = x * (1 + scale2) + shift2
        h = jnp.square(jax.nn.relu(jnp.einsum('bsd,df->bsf', u, mlp_w1[l]) + mlp_b1[l]))
        h = jnp.einsum('bsf,fd->bsd', h, mlp_w2[l]) + mlp_b2[l]
        x = layer_norm(DN_ALPHA * x + (1 + gate2) * h, ln2_g[l], ln2_b[l])
    return x
```

```python
import functools

import numpy as np
import jax
import jax.numpy as jnp
from jax import lax
from jax.experimental import pallas as pl
from jax.experimental.pallas import tpu as pltpu

F32 = jnp.float32
BF16 = jnp.bfloat16

HEAD_DIM = 128
GDN_HEADS = 16
GDN_CONV = 4
GDN_CHUNK = 64
NSA_HEADS = 16
NSA_GROUPS = 4
HPG = NSA_HEADS // NSA_GROUPS
CMP_BLOCK = 32
CMP_STRIDE = 16
SLC_BLOCK = 64
SLC_TOPK = 16
WINDOW = 512
N_ADA = 6
LN_EPS = 1e-5
RMS_EPS = 1e-6
NEG_INF = -1e30
FORCED_SCORE = 1e6

GDN_W = GDN_HEADS * HEAD_DIM
NSA_QW = NSA_HEADS * HEAD_DIM
NSA_KW = NSA_GROUPS * HEAD_DIM

C_QKV = 0
C_Z = 3 * GDN_W
C_NQ = C_Z + GDN_W
C_KV = C_NQ + NSA_QW
C_MA = C_KV + 6 * NSA_KW
SMALL_W = 128
SM_PB, SM_PA, SM_NG = 0, GDN_HEADS, 2 * GDN_HEADS
PROJ_TN = 512

VMEM_LIMIT = 56 * 1024 * 1024


def _cparams(sem):
    return pltpu.CompilerParams(dimension_semantics=sem, vmem_limit_bytes=VMEM_LIMIT)


def _sigmoid(x):
    return 1.0 / (1.0 + jnp.exp(-x))


def _silu(x):
    return x * _sigmoid(x)


def _dot(a, b):
    return jnp.dot(a, b, preferred_element_type=F32)


def _dot_nt(a, b):
    return lax.dot_general(a, b, (((1,), (1,)), ((), ())), preferred_element_type=F32)


def _split(a):
    hi = a.astype(BF16)
    lo = (a - hi.astype(F32)).astype(BF16)
    return hi, lo


def _dot3(a, b):
    ah, al = _split(a)
    bh, bl = _split(b)
    return _dot(ah, bh) + (_dot(ah, bl) + _dot(al, bh))


def _mm_kernel(*refs, nk, n_extra, epilogue):
    a_ref, b_ref = refs[0], refs[1]
    extra = refs[2:2 + n_extra]
    o_ref = refs[2 + n_extra]
    if nk == 1:
        acc = _dot(a_ref[...], b_ref[...])
        o_ref[...] = epilogue(acc, *[e[...] for e in extra]).astype(o_ref.dtype)
    else:
        acc_ref = refs[3 + n_extra]
        k = pl.program_id(2)

        @pl.when(k == 0)
        def _():
            acc_ref[...] = jnp.zeros_like(acc_ref)

        acc_ref[...] += _dot(a_ref[...], b_ref[...])

        @pl.when(k == nk - 1)
        def _():
            o_ref[...] = epilogue(acc_ref[...], *[e[...] for e in extra]).astype(o_ref.dtype)


def _matmul(a, b, *, tm, tn, tk=None, out_dtype=F32, extras=(), extra_specs=(), epilogue=None):
    m, kdim = a.shape
    n = b.shape[1]
    tm, tn = min(tm, m), min(tn, n)
    tk = kdim if tk is None else min(tk, kdim)
    nk = kdim // tk
    if epilogue is None:
        epilogue = lambda acc: acc
    kern = functools.partial(_mm_kernel, nk=nk, n_extra=len(extras), epilogue=epilogue)
    in_specs = [pl.BlockSpec((tm, tk), lambda i, j, k: (i, k)),
                pl.BlockSpec((tk, tn), lambda i, j, k: (k, j))] + list(extra_specs)
    scratch = [] if nk == 1 else [pltpu.VMEM((tm, tn), F32)]
    return pl.pallas_call(
        kern,
        grid=(m // tm, n // tn, nk),
        in_specs=in_specs,
        out_specs=pl.BlockSpec((tm, tn), lambda i, j, k: (i, j)),
        out_shape=jax.ShapeDtypeStruct((m, n), out_dtype),
        scratch_shapes=scratch,
        compiler_params=_cparams(("parallel", "parallel", "arbitrary")),
    )(a, b, *extras)


ADA_ROWS = 8
ADA_TN = 512


def _ada_kernel(c_ref, w_ref, b_ref, o_ref):
    o_ref[...] = _dot3(_silu(c_ref[...]), w_ref[...]) + b_ref[...]


def _ada_projection(c, ada_w, ada_b):
    batch, d = c.shape
    n = ada_w.shape[1]
    tn = min(ADA_TN, n)
    c_pad = jnp.zeros((ADA_ROWS, d), F32).at[:batch].set(c)
    out = pl.pallas_call(
        _ada_kernel,
        grid=(n // tn,),
        in_specs=[pl.BlockSpec((ADA_ROWS, d), lambda j: (0, 0)),
                  pl.BlockSpec((d, tn), lambda j: (0, j)),
                  pl.BlockSpec((1, tn), lambda j: (0, j))],
        out_specs=pl.BlockSpec((ADA_ROWS, tn), lambda j: (0, j)),
        out_shape=jax.ShapeDtypeStruct((ADA_ROWS, n), F32),
        compiler_params=_cparams(("parallel",)),
    )(c_pad, ada_w, ada_b.reshape(1, n))
    return out[:batch]
def _modulate_kernel(x_ref, sc_ref, sh_ref, u_ref):
    u_ref[...] = (x_ref[...] * (1.0 + sc_ref[0]) + sh_ref[0]).astype(u_ref.dtype)


def _modulate(x2, scale, shift, seq, ts):
    t, d = x2.shape
    ts = min(ts, seq)
    per_b = lambda i: ((i * ts) // seq, 0, 0)
    return pl.pallas_call(
        _modulate_kernel,
        grid=(t // ts,),
        in_specs=[pl.BlockSpec((ts, d), lambda i: (i, 0)),
                  pl.BlockSpec((1, 1, d), per_b),
                  pl.BlockSpec((1, 1, d), per_b)],
        out_specs=pl.BlockSpec((ts, d), lambda i: (i, 0)),
        out_shape=jax.ShapeDtypeStruct((t, d), BF16),
        compiler_params=_cparams(("parallel",)),
    )(x2, scale, shift)


def _ln_kernel(*refs, alpha, emit_u):
    if emit_u:
        x_ref, h_ref, gate_ref, g_ref, b_ref, sc_ref, sh_ref, xo_ref, uo_ref = refs
    else:
        x_ref, h_ref, gate_ref, g_ref, b_ref, xo_ref = refs
    v = alpha * x_ref[...] + (1.0 + gate_ref[0]) * h_ref[...]
    mu = jnp.mean(v, axis=-1, keepdims=True)
    vc = v - mu
    var = jnp.mean(vc * vc, axis=-1, keepdims=True)
    xn = vc * lax.rsqrt(var + LN_EPS) * g_ref[...] + b_ref[...]
    xo_ref[...] = xn
    if emit_u:
        uo_ref[...] = (xn * (1.0 + sc_ref[0]) + sh_ref[0]).astype(uo_ref.dtype)


def _deepnorm_ln(x2, h2, gate, ln_g, ln_b, seq, ts, alpha, next_scale=None, next_shift=None):
    t, d = x2.shape
    ts = min(ts, seq)
    emit_u = next_scale is not None
    per_b = lambda i: ((i * ts) // seq, 0, 0)
    row = pl.BlockSpec((ts, d), lambda i: (i, 0))
    vec_b = pl.BlockSpec((1, 1, d), per_b)
    vec = pl.BlockSpec((1, d), lambda i: (0, 0))
    in_specs = [row, row, vec_b, vec, vec]
    args = [x2, h2, gate, ln_g.reshape(1, d), ln_b.reshape(1, d)]
    out_specs = [row]
    out_shape = [jax.ShapeDtypeStruct((t, d), F32)]
    if emit_u:
        in_specs += [vec_b, vec_b]
        args += [next_scale, next_shift]
        out_specs.append(row)
        out_shape.append(jax.ShapeDtypeStruct((t, d), BF16))
    res = pl.pallas_call(
        functools.partial(_ln_kernel, alpha=alpha, emit_u=emit_u),
        grid=(t // ts,),
        in_specs=in_specs,
        out_specs=out_specs,
        out_shape=out_shape,
        compiler_params=_cparams(("parallel",)),
    )(*args)
    return (res[0], res[1]) if emit_u else (res[0], None)


GDN_PREP_TN = 512


def _gdn_prep_kernel(cur_ref, halo_ref, w_ref, o_ref, buf_ref, *, ts, seq):
    i = pl.program_id(0)
    j = pl.program_id(1)
    first = (i * ts) % seq == 0
    buf_ref[0:8, :] = jnp.where(first, 0.0, halo_ref[...])
    buf_ref[8:8 + ts, :] = cur_ref[...]
    w = w_ref[...]
    y = w[3:4] * buf_ref[8:8 + ts, :]
    for tap in range(GDN_CONV - 1):
        y = y + w[tap:tap + 1] * buf_ref[5 + tap:5 + tap + ts, :]
    y = _silu(y)
    heads_per_blk = GDN_PREP_TN // HEAD_DIM
    is_q = j < GDN_W // GDN_PREP_TN
    is_qk = j < 2 * GDN_W // GDN_PREP_TN
    for h in range(heads_per_blk):
        yh = y[:, h * HEAD_DIM:(h + 1) * HEAD_DIM]
        r = lax.rsqrt(jnp.sum(yh * yh, axis=-1, keepdims=True) + RMS_EPS)
        scale = jnp.where(is_q, r * HEAD_DIM ** -0.5, jnp.where(is_qk, r, 1.0))
        o_ref[:, h * HEAD_DIM:(h + 1) * HEAD_DIM] = yh * scale


def _gdn_prep(proj, conv_w, seq, ts):
    t = proj.shape[0]
    ts = min(ts, seq)
    tn = GDN_PREP_TN
    kern = functools.partial(_gdn_prep_kernel, ts=ts, seq=seq)
    return pl.pallas_call(
        kern,
        grid=(t // ts, 3 * GDN_W // tn),
        in_specs=[pl.BlockSpec((ts, tn), lambda i, j: (i, j)),
                  pl.BlockSpec((8, tn), lambda i, j: (jnp.maximum(i * (ts // 8) - 1, 0), j)),
                  pl.BlockSpec((GDN_CONV, tn), lambda i, j: (0, j))],
        out_specs=pl.BlockSpec((ts, tn), lambda i, j: (i, j)),
        out_shape=jax.ShapeDtypeStruct((t, 3 * GDN_W), F32),
        scratch_shapes=[pltpu.VMEM((ts + 8, tn), F32)],
        compiler_params=_cparams(("parallel", "parallel")),
    )(proj, proj, conv_w)


def _unit_lower_inverse(a, ri, ci):
    eye = (ri == ci).astype(F32)
    blk16 = (ri // 16) == (ci // 16)
    blk32 = (ri // 32) == (ci // 32)
    d = jnp.where(blk16, a, 0.0)
    x = eye - d
    p = _dot3(d, d)
    x = x + _dot3(x, p)
    p = _dot3(p, p)
    x = x + _dot3(x, p)
    p = _dot3(p, p)
    x = x + _dot3(x, p)
    e = jnp.where(jnp.logical_and(blk32, jnp.logical_not(blk16)), a, 0.0)
    x = x - _dot3(_dot3(x, e), x)
    e = jnp.where(blk32, 0.0, a)
    x = x - _dot3(_dot3(x, e), x)
    return x


def _gdn_chunk_kernel(q_ref, k_ref, v_ref, z_ref, pb_ref, pa_ref, alog_ref, dtb_ref, nw_ref, o_ref,
                      *, n_chunks):
    c_len = GDN_CHUNK
    ri = lax.broadcasted_iota(jnp.int32, (c_len, c_len), 0)
    ci = lax.broadcasted_iota(jnp.int32, (c_len, c_len), 1)
    eye = ri == ci
    incl = ri >= ci
    strict = ri > ci
    a_coef = -jnp.exp(alog_ref[0][:, :c_len])
    dtb = dtb_ref[0][:, :c_len]
    nw = nw_ref[...]

    def body(c, state):
        r0 = pl.multiple_of(c * c_len, c_len)
        qc = q_ref[pl.ds(r0, c_len), :]
        kc = k_ref[pl.ds(r0, c_len), :]
        vc = v_ref[pl.ds(r0, c_len), :]
        pa = pa_ref[0, pl.ds(c, 1), :] + dtb
        softplus = jnp.maximum(pa, 0.0) + jnp.log(1.0 + jnp.exp(-jnp.abs(pa)))
        la_row = a_coef * softplus
        beta_row = _sigmoid(pb_ref[0, pl.ds(c, 1), :])
        la_b = jnp.broadcast_to(la_row, (c_len, c_len))
        g_col = jnp.sum(jnp.where(incl, la_b, 0.0), axis=1, keepdims=True)
        la_col = jnp.sum(jnp.where(eye, la_b, 0.0), axis=1, keepdims=True)
        beta_col = jnp.sum(jnp.where(eye, jnp.broadcast_to(beta_row, (c_len, c_len)), 0.0),
                           axis=1, keepdims=True)
        g_row = jnp.sum(jnp.where(ci >= ri, jnp.broadcast_to(la_col, (c_len, c_len)), 0.0),
                        axis=0, keepdims=True)
        g_last = jnp.sum(la_row, axis=1, keepdims=True)
        decay = jnp.where(incl, jnp.exp(jnp.where(incl, g_col - g_row, 0.0)), 0.0)
        kk = _dot3(kc, kc.T)
        a_mat = jnp.where(strict, beta_col * kk * decay, 0.0)
        t_inv = _unit_lower_inverse(a_mat, ri, ci)
        w = _dot3(t_inv, (beta_col * jnp.exp(g_col)) * kc)
        u = _dot3(t_inv, beta_col * vc)
        qk = _dot_nt(qc.astype(BF16), kc.astype(BF16)) * decay
        k_end = kc * jnp.exp(g_last - g_col)
        q_g = qc * jnp.exp(g_col)
        s_bf = state.astype(BF16)
        u_corr = u - _dot(w.astype(BF16), s_bf)
        o = _dot(q_g.astype(BF16), s_bf) + _dot(qk.astype(BF16), u_corr.astype(BF16))
        state = jnp.exp(g_last) * state + _dot(k_end.T.astype(BF16), u_corr.astype(BF16))
        o = o * lax.rsqrt(jnp.mean(o * o, axis=-1, keepdims=True) + RMS_EPS) * nw
        o = o * _silu(z_ref[pl.ds(r0, c_len), :])
        o_ref[pl.ds(r0, c_len), :] = o.astype(o_ref.dtype)
        return state

    lax.fori_loop(0, n_chunks, body, jnp.zeros((HEAD_DIM, HEAD_DIM), F32))


def _gdn_chunk(qkv, proj, pb_t, pa_t, a_log, dt_bias, norm_w, batch, seq):
    t = qkv.shape[0]
    n_chunks = seq // GDN_CHUNK
    hb = GDN_W // HEAD_DIM
    blk = lambda off: pl.BlockSpec((seq, HEAD_DIM), lambda b, h: (b, off + h))
    row_spec = pl.BlockSpec((1, n_chunks, GDN_CHUNK), lambda b, h: (b * GDN_HEADS + h, 0, 0))
    head_vec = pl.BlockSpec((1, 1, HEAD_DIM), lambda b, h: (h, 0, 0))
    kern = functools.partial(_gdn_chunk_kernel, n_chunks=n_chunks)
    return pl.pallas_call(
        kern,
        grid=(batch, GDN_HEADS),
        in_specs=[blk(0), blk(hb), blk(2 * hb),
                  pl.BlockSpec((seq, HEAD_DIM), lambda b, h: (b, C_Z // HEAD_DIM + h)),
                  row_spec, row_spec, head_vec, head_vec,
                  pl.BlockSpec((1, HEAD_DIM), lambda b, h: (0, 0))],
        out_specs=pl.BlockSpec((seq, HEAD_DIM), lambda b, h: (b, h)),
        out_shape=jax.ShapeDtypeStruct((t, GDN_W), BF16),
        compiler_params=_cparams(("parallel", "parallel")),
    )(qkv, qkv, qkv, proj, pb_t, pa_t, a_log, dt_bias, norm_w)


def _cmp_kernel(seg_ref, w1_ref, pos_ref, w2_ref, o_ref):
    seg = seg_ref[0, 0, 0].astype(BF16)
    ns = seg.shape[0]
    half = CMP_STRIDE * HEAD_DIM
    w1 = w1_ref[0]
    first = _dot(seg, w1[:half])
    second = _dot(seg, w1[half:])
    pos = jnp.broadcast_to(pos_ref[0], (8, 2 * half)).astype(BF16)
    pos_term = _dot(pos, w1)[0:1]
    hid = _silu(first + pltpu.roll(second, ns - 1, axis=0) + pos_term)
    out = _dot(hid.astype(BF16), w2_ref[0])
    rows = lax.broadcasted_iota(jnp.int32, out.shape, 0)
    o_ref[0, 0, 0] = jnp.where(rows < ns - 1, out, 0.0)


def _nsa_compress(segs, w1, pos, w2):
    b, _, g, ns, width = segs.shape
    return pl.pallas_call(
        _cmp_kernel,
        grid=(b, 2, g),
        in_specs=[pl.BlockSpec((1, 1, 1, ns, width), lambda bi, kv, gi: (bi, kv, gi, 0, 0)),
                  pl.BlockSpec((1, 2 * width, HEAD_DIM), lambda bi, kv, gi: (kv, 0, 0)),
                  pl.BlockSpec((1, 1, 2 * width), lambda bi, kv, gi: (kv, 0, 0)),
                  pl.BlockSpec((1, HEAD_DIM, HEAD_DIM), lambda bi, kv, gi: (kv, 0, 0))],
        out_specs=pl.BlockSpec((1, 1, 1, ns, HEAD_DIM), lambda bi, kv, gi: (bi, kv, gi, 0, 0)),
        out_shape=jax.ShapeDtypeStruct((b, 2, g, ns, HEAD_DIM), F32),
        compiler_params=_cparams(("parallel", "parallel", "parallel")),
    )(segs, w1, pos, w2)


NSA_TQ = 128
NSA_KC = 512


def _masked_softmax(s, valid):
    s = jnp.where(valid, s, NEG_INF)
    m = jnp.max(s, axis=-1, keepdims=True)
    p = jnp.where(valid, jnp.exp(s - m), 0.0)
    l = jnp.sum(p, axis=-1, keepdims=True)
    return p / jnp.where(l > 0.0, l, 1.0)


def _nsa_kernel(q_ref, kc_ref, vc_ref, ks_ref, vs_ref, kw_ref, vw_ref, sm_ref, c2s_ref, o_ref,
                *, tq, seq, kchunk):
    g = pl.program_id(1)
    i = pl.program_id(2)
    t0 = i * tq
    rows = HPG * tq
    ns = kc_ref.shape[-2]
    nb = seq // SLC_BLOCK
    top_n = min(SLC_TOPK, nb)

    q = q_ref[...]
    qr = jnp.concatenate([q[:, h * HEAD_DIM:(h + 1) * HEAD_DIM] for h in range(HPG)], axis=0)
    qr = (qr * HEAD_DIM ** -0.5).astype(BF16)
    row = lax.broadcasted_iota(jnp.int32, (rows, 1), 0)
    hh = row // tq
    tf = (t0 + row - hh * tq).astype(F32)
    slope = jnp.exp2(-0.5 * (HPG * g + hh + 1).astype(F32))

    kcm = kc_ref[0, 0, 0].astype(BF16)
    vcm = vc_ref[0, 0, 0].astype(BF16)
    jj = lax.broadcasted_iota(jnp.int32, (1, ns), 1)
    dist = tf - (jj * CMP_STRIDE + (CMP_BLOCK - 1)).astype(F32)
    valid = jnp.logical_and(dist >= 0.0, jj < ns - 1)
    p_cmp = _masked_softmax(_dot_nt(qr, kcm) - slope * dist, valid)
    o_cmp = _dot(p_cmp.astype(BF16), vcm)
    p_sum = p_cmp[0:tq]
    for h in range(1, HPG):
        p_sum = p_sum + p_cmp[h * tq:(h + 1) * tq]
    p_hi, p_lo = _split(p_sum)
    c2s = c2s_ref[...]
    imp = _dot(p_hi, c2s) + _dot(p_lo, c2s)

    tt = t0 + lax.broadcasted_iota(jnp.int32, (tq, 1), 0)
    cur = tt // SLC_BLOCK
    blk = lax.broadcasted_iota(jnp.int32, (1, nb), 1)
    allowed = blk * SLC_BLOCK <= tt
    forced = jnp.logical_or(blk == 0, jnp.logical_or(blk == cur, blk == cur - 1))
    sc = jnp.where(forced, FORCED_SCORE, jnp.where(allowed, imp, NEG_INF))
    sel = jnp.zeros((tq, nb), F32)
    for _ in range(top_n):
        mx = jnp.max(sc, axis=-1, keepdims=True)
        idx = jnp.min(jnp.where(sc == mx, blk, nb), axis=-1, keepdims=True)
        pick = blk == idx
        sel = jnp.where(pick, 1.0, sel)
        sc = jnp.where(pick, -jnp.inf, sc)
    sel = jnp.where(allowed, sel, 0.0)
    sel_rows = jnp.concatenate([sel] * HPG, axis=0).astype(BF16)

    blk_col = lax.broadcasted_iota(jnp.int32, (nb, 1), 0)
    key_lane = lax.broadcasted_iota(jnp.int32, (1, kchunk), 1)

    def slc_body(c, carry):
        m, l, acc = carry
        k0 = pl.multiple_of(c * kchunk, kchunk)
        kb = ks_ref[pl.ds(k0, kchunk), :].astype(BF16)
        vb = vs_ref[pl.ds(k0, kchunk), :].astype(BF16)
        pos = k0 + key_lane
        d = tf - pos.astype(F32)
        expand = (blk_col == pos // SLC_BLOCK).astype(BF16)
        picked = _dot(sel_rows, expand)
        ok = jnp.logical_and(picked > 0.5, d >= 0.0)
        s = jnp.where(ok, _dot_nt(qr, kb) - slope * d, NEG_INF)
        m_new = jnp.maximum(m, jnp.max(s, axis=-1, keepdims=True))
        alpha = jnp.exp(m - m_new)
        p = jnp.where(ok, jnp.exp(s - m_new), 0.0)
        l = alpha * l + jnp.sum(p, axis=-1, keepdims=True)
        acc = alpha * acc + _dot(p.astype(BF16), vb)
        return m_new, l, acc

    n_kc = (t0 + tq + kchunk - 1) // kchunk
    _, l_s, acc_s = lax.fori_loop(
        0, n_kc, slc_body,
        (jnp.full((rows, 1), NEG_INF, F32), jnp.zeros((rows, 1), F32), jnp.zeros((rows, HEAD_DIM), F32)))
    o_slc = acc_s / jnp.where(l_s > 0.0, l_s, 1.0)

    wk = WINDOW + tq
    w0 = pl.multiple_of(jnp.maximum(t0 - WINDOW, 0), tq)
    kb = kw_ref[pl.ds(w0, wk), :].astype(BF16)
    vb = vw_ref[pl.ds(w0, wk), :].astype(BF16)
    pos = w0 + lax.broadcasted_iota(jnp.int32, (1, wk), 1)
    d = tf - pos.astype(F32)
    ok = jnp.logical_and(d >= 0.0, d < float(WINDOW))
    p_win = _masked_softmax(_dot_nt(qr, kb) - slope * d, ok)
    o_win = _dot(p_win.astype(BF16), vb)

    gt = _sigmoid(sm_ref[...])
    lane = lax.broadcasted_iota(jnp.int32, (1, SMALL_W), 1)
    branches = (o_cmp, o_slc, o_win)
    for h in range(HPG):
        out_h = jnp.zeros((tq, HEAD_DIM), F32)
        for br in range(3):
            col = SM_NG + br * NSA_HEADS + g * HPG + h
            gate = jnp.sum(jnp.where(lane == col, gt, 0.0), axis=-1, keepdims=True)
            out_h = out_h + gate * branches[br][h * tq:(h + 1) * tq]
        o_ref[:, h * HEAD_DIM:(h + 1) * HEAD_DIM] = out_h.astype(o_ref.dtype)


def _nsa_attend(proj, cmp_kv, c2s, c_small, batch, seq):
    t = proj.shape[0]
    tq = min(NSA_TQ, seq)
    kchunk = min(NSA_KC, seq)
    nq = seq // tq
    ns = cmp_kv.shape[-2]
    nb = seq // SLC_BLOCK
    kv_blk = lambda idx: pl.BlockSpec(
        (seq, HEAD_DIM), lambda b, g, i: (b, (C_KV + idx * NSA_KW) // HEAD_DIM + g))
    cmp_blk = lambda kv: pl.BlockSpec((1, 1, 1, ns, HEAD_DIM), lambda b, g, i: (b, kv, g, 0, 0))
    kern = functools.partial(_nsa_kernel, tq=tq, seq=seq, kchunk=kchunk)
    return pl.pallas_call(
        kern,
        grid=(batch, NSA_GROUPS, nq),
        in_specs=[pl.BlockSpec((tq, NSA_KW), lambda b, g, i: (b * nq + i, C_NQ // NSA_KW + g)),
                  cmp_blk(0), cmp_blk(1),
                  kv_blk(2), kv_blk(3), kv_blk(4), kv_blk(5),
                  pl.BlockSpec((tq, SMALL_W), lambda b, g, i: (b * nq + i, c_small // SMALL_W)),
                  pl.BlockSpec((ns, nb), lambda b, g, i: (0, 0))],
        out_specs=pl.BlockSpec((tq, NSA_KW), lambda b, g, i: (b * nq + i, g)),
        out_shape=jax.ShapeDtypeStruct((t, NSA_QW), BF16),
        compiler_params=_cparams(("parallel", "parallel", "arbitrary")),
    )(proj, cmp_kv, cmp_kv, proj, proj, proj, proj, proj, c2s)


def _readout_kernel(oa_ref, ob_ref, wa_ref, wb_ref, ma_ref, mb_ref, y_ref):
    ya = _dot(oa_ref[...], wa_ref[...])
    yb = _dot(ob_ref[...], wb_ref[...])
    y_ref[...] = (_sigmoid(ma_ref[...]) * ya + _sigmoid(mb_ref[...]) * yb).astype(y_ref.dtype)


def _readout(o_a, o_b, w_a, w_b, proj, d_model, tm, tn):
    t = o_a.shape[0]
    tm, tn = min(tm, t), min(tn, d_model)
    c_mb = C_MA + d_model
    return pl.pallas_call(
        _readout_kernel,
        grid=(t // tm, d_model // tn),
        in_specs=[pl.BlockSpec((tm, GDN_W), lambda i, j: (i, 0)),
                  pl.BlockSpec((tm, NSA_QW), lambda i, j: (i, 0)),
                  pl.BlockSpec((GDN_W, tn), lambda i, j: (0, j)),
                  pl.BlockSpec((NSA_QW, tn), lambda i, j: (0, j)),
                  pl.BlockSpec((tm, tn), lambda i, j: (i, C_MA // tn + j)),
                  pl.BlockSpec((tm, tn), lambda i, j: (i, c_mb // tn + j))],
        out_specs=pl.BlockSpec((tm, tn), lambda i, j: (i, j)),
        out_shape=jax.ShapeDtypeStruct((t, d_model), BF16),
        compiler_params=_cparams(("parallel", "parallel")),
    )(o_a, o_b, w_a, w_b, proj, proj)


def _cmp_to_slc(seq):
    ns = seq // CMP_STRIDE
    nb = seq // SLC_BLOCK
    start = np.arange(ns) * CMP_STRIDE
    end = start + CMP_BLOCK - 1
    s0 = np.arange(nb) * SLC_BLOCK
    m = (end[:, None] >= s0[None, :]) & (start[:, None] <= s0[None, :] + SLC_BLOCK - 1)
    m[ns - 1] = False
    return jnp.asarray(m, dtype=BF16)


def _permute_w_in(w, d_model):
    g0 = 4 * GDN_W
    s0 = g0 + 2 * GDN_HEADS
    s1 = s0 + NSA_QW + 6 * NSA_KW
    s2 = s1 + 3 * NSA_HEADS
    n_main = C_MA + 2 * d_model
    total = -(-(n_main + SMALL_W) // PROJ_TN) * PROJ_TN
    pad = total - n_main - 2 * GDN_HEADS - 3 * NSA_HEADS
    parts = [w[:, :g0], w[:, s0:s1], w[:, s2:], w[:, g0:s0], w[:, s1:s2],
             jnp.zeros((w.shape[0], pad), w.dtype)]
    return jnp.concatenate(parts, axis=1).astype(BF16)


def _mixer(u, proj_w, conv_w, a_log, dt_bias, norm_w, cmp_pos, cmp_w1, cmp_w2, w_read_a, w_read_b,
           w_out, c2s, batch, seq, d_model):
    t = batch * seq
    c_small = C_MA + 2 * d_model
    proj = _matmul(u, proj_w, tm=1024, tn=PROJ_TN)

    qkv = _gdn_prep(proj, conv_w, seq, ts=512)
    small = proj[:, c_small:c_small + SMALL_W].reshape(batch, seq, SMALL_W)
    n_chunks = seq // GDN_CHUNK

    def head_rows(cols):
        return jnp.transpose(cols, (0, 2, 1)).reshape(batch * GDN_HEADS, n_chunks, GDN_CHUNK)

    pb_t = head_rows(small[:, :, SM_PB:SM_PB + GDN_HEADS])
    pa_t = head_rows(small[:, :, SM_PA:SM_PA + GDN_HEADS])
    bcast = lambda v: jnp.broadcast_to(v.reshape(GDN_HEADS, 1, 1), (GDN_HEADS, 1, HEAD_DIM))
    o_a = _gdn_chunk(qkv, proj, pb_t, pa_t, bcast(a_log), bcast(dt_bias), norm_w.reshape(1, HEAD_DIM),
                     batch, seq)

    ns = seq // CMP_STRIDE

    def segments(c0):
        cols = proj[:, c0:c0 + NSA_KW].reshape(batch, ns, CMP_STRIDE, NSA_GROUPS, HEAD_DIM)
        return jnp.transpose(cols, (0, 3, 1, 2, 4)).reshape(batch, NSA_GROUPS, ns, CMP_STRIDE * HEAD_DIM)

    segs = jnp.stack([segments(C_KV), segments(C_KV + NSA_KW)], axis=1)
    cmp_kv = _nsa_compress(segs,
                           cmp_w1.reshape(2, CMP_BLOCK * HEAD_DIM, HEAD_DIM).astype(BF16),
                           cmp_pos.reshape(2, 1, CMP_BLOCK * HEAD_DIM),
                           cmp_w2.astype(BF16))
    o_b = _nsa_attend(proj, cmp_kv, c2s, c_small, batch, seq)

    y = _readout(o_a, o_b, w_read_a.astype(BF16), w_read_b.astype(BF16), proj, d_model, tm=512, tn=512)
    return _matmul(y, w_out.astype(BF16), tm=1024, tn=512)


def kernel(x, c, ada_w, ada_b, ada_table, w_in, gdn_conv_w, gdn_a_log, gdn_dt_bias, gdn_norm_w,
           cmp_pos, cmp_w1, cmp_w2, w_read_a, w_read_b, w_out, ln1_g, ln1_b,
           mlp_w1, mlp_b1, mlp_w2, mlp_b2, ln2_g, ln2_b):
    batch, seq, d_model = x.shape
    depth = w_in.shape[0]
    d_ff = mlp_w1.shape[-1]
    t = batch * seq
    alpha = (2.0 * depth) ** 0.25
    c2s = _cmp_to_slc(seq)

    mod = _ada_projection(c, ada_w, ada_b)

    x2 = x.reshape(t, d_model)
    u = None
    for l in range(depth):
        m = (mod + ada_table[l].reshape(1, -1)).reshape(batch, N_ADA, 1, d_model)
        shift1, scale1, gate1, shift2, scale2, gate2 = (m[:, j] for j in range(N_ADA))
        if u is None:
            u = _modulate(x2, scale1, shift1, seq, ts=512)
        h = _mixer(u, _permute_w_in(w_in[l], d_model), gdn_conv_w[l], gdn_a_log[l], gdn_dt_bias[l],
                   gdn_norm_w[l], cmp_pos[l], cmp_w1[l], cmp_w2[l], w_read_a[l], w_read_b[l], w_out[l],
                   c2s, batch, seq, d_model)
        x2, u = _deepnorm_ln(x2, h, gate1, ln1_g[l], ln1_b[l], seq, 256, alpha, scale2, shift2)

        tn1 = min(1024, d_ff)
        act = _matmul(u, mlp_w1[l].astype(BF16), tm=1024, tn=tn1, out_dtype=BF16,
                      extras=(mlp_b1[l].reshape(1, d_ff),),
                      extra_specs=(pl.BlockSpec((1, tn1), lambda i, j, k: (0, j)),),
                      epilogue=lambda acc, b: jnp.square(jnp.maximum(acc + b, 0.0)))
        tn2 = min(1024, d_model)
        h = _matmul(act, mlp_w2[l].astype(BF16), tm=1024, tn=tn2, tk=2048,
                    extras=(mlp_b2[l].reshape(1, d_model),),
                    extra_specs=(pl.BlockSpec((1, tn2), lambda i, j, k: (0, j)),),
                    epilogue=lambda acc, b: acc + b)
        if l + 1 < depth:
            m_next = (mod + ada_table[l + 1].reshape(1, -1)).reshape(batch, N_ADA, 1, d_model)
            x2, u = _deepnorm_ln(x2, h, gate2, ln2_g[l], ln2_b[l], seq, 256, alpha,
                                 m_next[:, 1], m_next[:, 0])
        else:
            x2, _ = _deepnorm_ln(x2, h, gate2, ln2_g[l], ln2_b[l], seq, 256, alpha)
    return x2.reshape(batch, seq, d_model)
```

```python
import functools

import numpy as np
import jax
import jax.numpy as jnp
from jax import lax
from jax.experimental import pallas as pl
from jax.experimental.pallas import tpu as pltpu

F32 = jnp.float32
BF16 = jnp.bfloat16

HEAD_DIM = 128
GDN_HEADS = 16
GDN_CONV = 4
GDN_CHUNK = 64
NSA_HEADS = 16
NSA_GROUPS = 4
HPG = NSA_HEADS // NSA_GROUPS
CMP_BLOCK = 32
CMP_STRIDE = 16
SLC_BLOCK = 64
SLC_TOPK = 16
WINDOW = 512
N_ADA = 6
LN_EPS = 1e-5
RMS_EPS = 1e-6
NEG_INF = -1e30
FORCED_SCORE = 1e6

GDN_W = GDN_HEADS * HEAD_DIM
NSA_QW = NSA_HEADS * HEAD_DIM
NSA_KW = NSA_GROUPS * HEAD_DIM

C_QKV = 0
C_Z = 3 * GDN_W
C_NQ = C_Z + GDN_W
C_KV = C_NQ + NSA_QW
C_MA = C_KV + 6 * NSA_KW
SMALL_W = 128
SM_PB, SM_PA, SM_NG = 0, GDN_HEADS, 2 * GDN_HEADS
PROJ_TN = 512

VMEM_LIMIT = 56 * 1024 * 1024


def _cparams(sem):
    return pltpu.CompilerParams(dimension_semantics=sem, vmem_limit_bytes=VMEM_LIMIT)


def _sigmoid(x):
    return 1.0 / (1.0 + jnp.exp(-x))


def _silu(x):
    return x * _sigmoid(x)


def _dot(a, b):
    return jnp.dot(a, b, preferred_element_type=F32)


def _dot_nt(a, b):
    return lax.dot_general(a, b, (((1,), (1,)), ((), ())), preferred_element_type=F32)


def _split(a):
    hi = a.astype(BF16)
    lo = (a - hi.astype(F32)).astype(BF16)
    return hi, lo


def _dot3(a, b):
    ah, al = _split(a)
    bh, bl = _split(b)
    return _dot(ah, bh) + (_dot(ah, bl) + _dot(al, bh))


def _mm_kernel(*refs, nk, n_extra, epilogue):
    a_ref, b_ref = refs[0], refs[1]
    extra = refs[2:2 + n_extra]
    o_ref = refs[2 + n_extra]
    if nk == 1:
        acc = _dot(a_ref[...], b_ref[...])
        o_ref[...] = epilogue(acc, *[e[...] for e in extra]).astype(o_ref.dtype)
    else:
        acc_ref = refs[3 + n_extra]
        k = pl.program_id(2)

        @pl.when(k == 0)
        def _():
            acc_ref[...] = jnp.zeros_like(acc_ref)

        acc_ref[...] += _dot(a_ref[...], b_ref[...])

        @pl.when(k == nk - 1)
        def _():
            o_ref[...] = epilogue(acc_ref[...], *[e[...] for e in extra]).astype(o_ref.dtype)


def _matmul(a, b, *, tm, tn, tk=None, out_dtype=F32, extras=(), extra_specs=(), epilogue=None):
    m, kdim = a.shape
    n = b.shape[1]
    tm, tn = min(tm, m), min(tn, n)
    tk = kdim if tk is None else min(tk, kdim)
    nk = kdim // tk
    if epilogue is None:
        epilogue = lambda acc: acc
    kern = functools.partial(_mm_kernel, nk=nk, n_extra=len(extras), epilogue=epilogue)
    in_specs = [pl.BlockSpec((tm, tk), lambda i, j, k: (i, k)),
                pl.BlockSpec((tk, tn), lambda i, j, k: (k, j))] + list(extra_specs)
    scratch = [] if nk == 1 else [pltpu.VMEM((tm, tn), F32)]
    return pl.pallas_call(
        kern,
        grid=(m // tm, n // tn, nk),
        in_specs=in_specs,
        out_specs=pl.BlockSpec((tm, tn), lambda i, j, k: (i, j)),
        out_shape=jax.ShapeDtypeStruct((m, n), out_dtype),
        scratch_shapes=scratch,
        compiler_params=_cparams(("parallel", "parallel", "arbitrary")),
    )(a, b, *extras)


ADA_ROWS = 8
ADA_TN = 512


def _ada_kernel(c_ref, w_ref, b_ref, o_ref):
    o_ref[...] = _dot3(_silu(c_ref[...]), w_ref[...]) + b_ref[...]


def _ada_projection(c, ada_w, ada_b):
    batch, d = c.shape
    n = ada_w.shape[1]
    tn = min(ADA_TN, n)
    assert batch <= ADA_ROWS
    c_pad = jnp.zeros((ADA_ROWS, d), F32).at[:batch].set(c)
    out = pl.pallas_call(
        _ada_kernel,
        grid=(n // tn,),
        in_specs=[pl.BlockSpec((ADA_ROWS, d), lambda j: (0, 0)),
                  pl.BlockSpec((d, tn), lambda j: (0, j)),
                  pl.BlockSpec((1, tn), lambda j: (0, j))],
        out_specs=pl.BlockSpec((ADA_ROWS, tn), lambda j: (0, j)),
        out_shape=jax.ShapeDtypeStruct((ADA_ROWS, n), F32),
        compiler_params=_cparams(("parallel",)),
    )(c_pad, ada_w, ada_b.reshape(1, n))
    return out[:batch]


def _modulate_kernel(x_ref, sc_ref, sh_ref, u_ref):
    u_ref[...] = (x_ref[...] * (1.0 + sc_ref[0]) + sh_ref[0]).astype(u_ref.dtype)


def _modulate(x2, scale, shift, seq, ts):
    t, d = x2.shape
    ts = min(ts, seq)
    per_b = lambda i: ((i * ts) // seq, 0, 0)
    return pl.pallas_call(
        _modulate_kernel,
        grid=(t // ts,),
        in_specs=[pl.BlockSpec((ts, d), lambda i: (i, 0)),
                  pl.BlockSpec((1, 1, d), per_b),
                  pl.BlockSpec((1, 1, d), per_b)],
        out_specs=pl.BlockSpec((ts, d), lambda i: (i, 0)),
        out_shape=jax.ShapeDtypeStruct((t, d), BF16),
        compiler_params=_cparams(("parallel",)),
    )(x2, scale, shift)


def _ln_kernel(*refs, alpha, emit_u):
    if emit_u:
        x_ref, h_ref, gate_ref, g_ref, b_ref, sc_ref, sh_ref, xo_ref, uo_ref = refs
    else:
        x_ref, h_ref, gate_ref, g_ref, b_ref, xo_ref = refs
    v = alpha * x_ref[...] + (1.0 + gate_ref[0]) * h_ref[...]
    mu = jnp.mean(v, axis=-1, keepdims=True)
    vc = v - mu
    var = jnp.mean(vc * vc, axis=-1, keepdims=True)
    xn = vc * lax.rsqrt(var + LN_EPS) * g_ref[...] + b_ref[...]
    xo_ref[...] = xn
    if emit_u:
        uo_ref[...] = (xn * (1.0 + sc_ref[0]) + sh_ref[0]).astype(uo_ref.dtype)


def _deepnorm_ln(x2, h2, gate, ln_g, ln_b, seq, ts, alpha, next_scale=None, next_shift=None):
    t, d = x2.shape
    ts = min(ts, seq)
    emit_u = next_scale is not None
    per_b = lambda i: ((i * ts) // seq, 0, 0)
    row = pl.BlockSpec((ts, d), lambda i: (i, 0))
    vec_b = pl.BlockSpec((1, 1, d), per_b)
    vec = pl.BlockSpec((1, d), lambda i: (0, 0))
    in_specs = [row, row, vec_b, vec, vec]
    args = [x2, h2, gate, ln_g.reshape(1, d), ln_b.reshape(1, d)]
    out_specs = [row]
    out_shape = [jax.ShapeDtypeStruct((t, d), F32)]
    if emit_u:
        in_specs += [vec_b, vec_b]
        args += [next_scale, next_shift]
        out_specs.append(row)
        out_shape.append(jax.ShapeDtypeStruct((t, d), BF16))
    res = pl.pallas_call(
        functools.partial(_ln_kernel, alpha=alpha, emit_u=emit_u),
        grid=(t // ts,),
        in_specs=in_specs,
        out_specs=out_specs,
        out_shape=out_shape,
        compiler_params=_cparams(("parallel",)),
    )(*args)
    return (res[0], res[1]) if emit_u else (res[0], None)


GDN_PREP_TN = 512


def _gdn_prep_kernel(cur_ref, halo_ref, w_ref, o_ref, buf_ref, *, ts, seq):
    i = pl.program_id(0)
    j = pl.program_id(1)
    first = (i * ts) % seq == 0
    buf_ref[0:8, :] = jnp.where(first, 0.0, halo_ref[...])
    buf_ref[8:8 + ts, :] = cur_ref[...]
    w = w_ref[...]
    y = w[3:4] * buf_ref[8:8 + ts, :]
    for tap in range(GDN_CONV - 1):
        y = y + w[tap:tap + 1] * buf_ref[5 + tap:5 + tap + ts, :]
    y = _silu(y)
    heads_per_blk = GDN_PREP_TN // HEAD_DIM
    is_q = j < GDN_W // GDN_PREP_TN
    is_qk = j < 2 * GDN_W // GDN_PREP_TN
    for h in range(heads_per_blk):
        yh = y[:, h * HEAD_DIM:(h + 1) * HEAD_DIM]
        r = lax.rsqrt(jnp.sum(yh * yh, axis=-1, keepdims=True) + RMS_EPS)
        scale = jnp.where(is_q, r * HEAD_DIM ** -0.5, jnp.where(is_qk, r, 1.0))
        o_ref[:, h * HEAD_DIM:(h + 1) * HEAD_DIM] = (yh * scale).astype(o_ref.dtype)


def _gdn_prep(proj, conv_w, seq, ts):
    t = proj.shape[0]
    ts = min(ts, seq)
    tn = GDN_PREP_TN
    kern = functools.partial(_gdn_prep_kernel, ts=ts, seq=seq)
    return pl.pallas_call(
        kern,
        grid=(t // ts, 3 * GDN_W // tn),
        in_specs=[pl.BlockSpec((ts, tn), lambda i, j: (i, j)),
                  pl.BlockSpec((8, tn), lambda i, j: (jnp.maximum(i * (ts // 8) - 1, 0), j)),
                  pl.BlockSpec((GDN_CONV, tn), lambda i, j: (0, j))],
        out_specs=pl.BlockSpec((ts, tn), lambda i, j: (i, j)),
        out_shape=jax.ShapeDtypeStruct((t, 3 * GDN_W), BF16),
        scratch_shapes=[pltpu.VMEM((ts + 8, tn), F32)],
        compiler_params=_cparams(("parallel", "parallel")),
    )(proj, proj, conv_w)


GDN_GROUP = 256
GDN_HB = 2
GDN_NG = 2


def _bmm(a, b):
    return lax.dot_general(a, b, (((2,), (1,)), ((0,), (0,))), preferred_element_type=F32)


def _bmm_nt(a, b):
    return lax.dot_general(a, b, (((2,), (2,)), ((0,), (0,))), preferred_element_type=F32)


def _unit_lower_inverse(a, ri, ci):
    grp = a.shape[-1]
    eye = (ri == ci).astype(F32)
    blk16 = (ri // 16) == (ci // 16)
    blk32 = (ri // 32) == (ci // 32)
    d = jnp.where(blk16, a, 0.0)
    x = eye - d
    d_bf = d.astype(BF16)
    p = _bmm(d_bf, d_bf)
    for _ in range(2):
        p_bf = p.astype(BF16)
        r = _bmm(jnp.concatenate([x.astype(BF16), p_bf], axis=1), p_bf)
        x = x + r[:, :grp]
        p = r[:, grp:]
    x = x + _bmm(x.astype(BF16), p.astype(BF16))
    for e in (jnp.where(blk16, 0.0, jnp.where(blk32, a, 0.0)), jnp.where(blk32, 0.0, a)):
        x_bf = x.astype(BF16)
        x = x - _bmm(_bmm(x_bf, e.astype(BF16)).astype(BF16), x_bf)
    return x


def _gdn_chunk_kernel(q_ref, k_ref, v_ref, z_ref, pb_ref, pa_ref, alog_ref, dtb_ref, nw_ref, o_ref,
                      *, n_trips):
    grp, c_len, hb, ng = GDN_GROUP, GDN_CHUNK, GDN_HB, GDN_NG
    cpg = grp // c_len
    span = ng * grp
    ri = lax.broadcasted_iota(jnp.int32, (grp, grp), 0)
    ci = lax.broadcasted_iota(jnp.int32, (grp, grp), 1)
    same = (ri // c_len) == (ci // c_len)
    eye = ri == ci
    incl = jnp.logical_and(same, ri >= ci)
    incl_t = jnp.logical_and(same, ci >= ri)
    col_chunk = lax.broadcasted_iota(jnp.int32, (HEAD_DIM, grp), 1) // c_len
    nw = nw_ref[...]
    a_coef = jnp.stack([-jnp.exp(alog_ref[hh][:, 0:1]) for _ in range(ng) for hh in range(hb)])
    dtb = jnp.stack([dtb_ref[hh][:, 0:1] for _ in range(ng) for hh in range(hb)])

    def body(it, state):
        r0 = pl.multiple_of(it * span, span)

        def load(ref):
            return jnp.stack([ref[pl.ds(r0 + gg * grp, grp), hh * HEAD_DIM:(hh + 1) * HEAD_DIM]
                              for gg in range(ng) for hh in range(hb)])

        def load_row(ref):
            return jnp.stack([ref[hh, pl.ds(it * ng + gg, 1), :] for gg in range(ng) for hh in range(hb)])

        q_b, k_b, v_b = load(q_ref), load(k_ref), load(v_ref)
        kf = k_b.astype(F32)
        pa = load_row(pa_ref) + dtb
        softplus = jnp.maximum(pa, 0.0) + jnp.log(1.0 + jnp.exp(-jnp.abs(pa)))
        la_row = a_coef * softplus
        beta_row = _sigmoid(load_row(pb_ref))
        nb = la_row.shape[0]
        la_b = jnp.broadcast_to(la_row, (nb, grp, grp))
        g_col = jnp.sum(jnp.where(incl, la_b, 0.0), axis=2, keepdims=True)
        gl_col = jnp.sum(jnp.where(same, la_b, 0.0), axis=2, keepdims=True)
        la_col = jnp.sum(jnp.where(eye, la_b, 0.0), axis=2, keepdims=True)
        beta_col = jnp.sum(jnp.where(eye, jnp.broadcast_to(beta_row, (nb, grp, grp)), 0.0),
                           axis=2, keepdims=True)
        g_row = jnp.sum(jnp.where(incl_t, jnp.broadcast_to(la_col, (nb, grp, grp)), 0.0),
                        axis=1, keepdims=True)
        decay = jnp.where(incl, jnp.exp(jnp.where(incl, g_col - g_row, 0.0)), 0.0)
        kq = _bmm_nt(jnp.concatenate([k_b, q_b], axis=1), k_b)
        a_mat = jnp.where(eye, 0.0, beta_col * kq[:, :grp] * decay)
        t_inv = _unit_lower_inverse(a_mat, ri, ci).astype(BF16)
        rhs = jnp.concatenate([(beta_col * jnp.exp(g_col)) * kf, beta_col * v_b.astype(F32)], axis=2)
        wu = _bmm(t_inv, rhs.astype(BF16)).astype(BF16)
        qk = (kq[:, grp:] * decay).astype(BF16)
        qk_wu = _bmm(qk, wu)
        q_eff = (q_b.astype(F32) * jnp.exp(g_col) - qk_wu[:, :, :HEAD_DIM]).astype(BF16)
        o_loc = qk_wu[:, :, HEAD_DIM:]
        k_end = (kf * jnp.exp(gl_col - g_col)).astype(BF16)
        k_end_t = jnp.stack([k_end[b].T for b in range(nb)])
        chunk_decay = jnp.exp(gl_col)
        trans = [_bmm(jnp.where(col_chunk == cc, k_end_t, jnp.zeros_like(k_end_t)), wu) for cc in range(cpg)]

        outs = []
        for gg in range(ng):
            ent = slice(gg * hb, (gg + 1) * hb)
            for cc in range(cpg):
                rows = slice(cc * c_len, (cc + 1) * c_len)
                lhs = jnp.concatenate([q_eff[ent, rows], trans[cc][ent, :, :HEAD_DIM].astype(BF16)], axis=1)
                r = _bmm(lhs, state.astype(BF16))
                outs.append(r[:, :c_len] + o_loc[ent, rows])
                state = (chunk_decay[ent, cc * c_len:cc * c_len + 1] * state - r[:, c_len:]
                         + trans[cc][ent, :, HEAD_DIM:])
        o = jnp.concatenate(outs, axis=1)
        o = o * lax.rsqrt(jnp.mean(o * o, axis=-1, keepdims=True) + RMS_EPS) * nw
        o_all = jnp.concatenate([o[hh] for hh in range(hb)], axis=1) * _silu(z_ref[pl.ds(r0, span), :])
        o_ref[pl.ds(r0, span), :] = o_all.astype(o_ref.dtype)
        return state

    lax.fori_loop(0, n_trips, body, jnp.zeros((hb, HEAD_DIM, HEAD_DIM), F32))


def _gdn_chunk(qkv, proj, pb_t, pa_t, a_log, dt_bias, norm_w, batch, seq):
    t = qkv.shape[0]
    n_groups = seq // GDN_GROUP
    width = GDN_HB * HEAD_DIM
    nhb = GDN_HEADS // GDN_HB
    blk = lambda off: pl.BlockSpec((seq, width), lambda b, h: (b, off + h))
    row_spec = pl.BlockSpec((GDN_HB, n_groups, GDN_GROUP), lambda b, h: (b * nhb + h, 0, 0))
    head_vec = pl.BlockSpec((GDN_HB, 1, HEAD_DIM), lambda b, h: (h, 0, 0))
    assert n_groups % GDN_NG == 0
    kern = functools.partial(_gdn_chunk_kernel, n_trips=n_groups // GDN_NG)
    return pl.pallas_call(
        kern,
        grid=(batch, nhb),
        in_specs=[blk(0), blk(nhb), blk(2 * nhb),
                  pl.BlockSpec((seq, width), lambda b, h: (b, C_Z // width + h)),
                  row_spec, row_spec, head_vec, head_vec,
                  pl.BlockSpec((1, HEAD_DIM), lambda b, h: (0, 0))],
        out_specs=pl.BlockSpec((seq, width), lambda b, h: (b, h)),
        out_shape=jax.ShapeDtypeStruct((t, GDN_W), BF16),
        compiler_params=_cparams(("parallel", "parallel")),
    )(qkv, qkv, qkv, proj, pb_t, pa_t, a_log, dt_bias, norm_w)


def _cmp_kernel(seg_ref, w1_ref, pos_ref, w2_ref, o_ref):
    seg = seg_ref[0, 0, 0].astype(BF16)
    ns = seg.shape[0]
    half = CMP_STRIDE * HEAD_DIM
    w1 = w1_ref[0]
    first = _dot(seg, w1[:half])
    second = _dot(seg, w1[half:])
    pos = jnp.broadcast_to(pos_ref[0], (8, 2 * half)).astype(BF16)
    pos_term = _dot(pos, w1)[0:1]
    hid = _silu(first + pltpu.roll(second, ns - 1, axis=0) + pos_term)
    out = _dot(hid.astype(BF16), w2_ref[0])
    rows = lax.broadcasted_iota(jnp.int32, out.shape, 0)
    o_ref[0, 0, 0] = jnp.where(rows < ns - 1, out, 0.0)


def _nsa_compress(segs, w1, pos, w2):
    b, _, g, ns, width = segs.shape
    return pl.pallas_call(
        _cmp_kernel,
        grid=(b, 2, g),
        in_specs=[pl.BlockSpec((1, 1, 1, ns, width), lambda bi, kv, gi: (bi, kv, gi, 0, 0)),
                  pl.BlockSpec((1, 2 * width, HEAD_DIM), lambda bi, kv, gi: (kv, 0, 0)),
                  pl.BlockSpec((1, 1, 2 * width), lambda bi, kv, gi: (kv, 0, 0)),
                  pl.BlockSpec((1, HEAD_DIM, HEAD_DIM), lambda bi, kv, gi: (kv, 0, 0))],
        out_specs=pl.BlockSpec((1, 1, 1, ns, HEAD_DIM), lambda bi, kv, gi: (bi, kv, gi, 0, 0)),
        out_shape=jax.ShapeDtypeStruct((b, 2, g, ns, HEAD_DIM), F32),
        compiler_params=_cparams(("parallel", "parallel", "parallel")),
    )(segs, w1, pos, w2)


NSA_TQ = 128
NSA_KC = 512
NSA_NBP = 128
MASK_BIG = 1e30


def _masked_softmax(s, valid):
    s = jnp.where(valid, s, NEG_INF)
    m = jnp.max(s, axis=-1, keepdims=True)
    p = jnp.where(valid, jnp.exp(s - m), 0.0)
    l = jnp.sum(p, axis=-1, keepdims=True)
    return p / jnp.where(l > 0.0, l, 1.0)


def _nsa_kernel(q_ref, kc_ref, vc_ref, ks_ref, vs_ref, kw_ref, vw_ref, sm_ref, c2s_ref, o_ref,
                *, tq, seq, kchunk):
    g = pl.program_id(1)
    i = pl.program_id(2)
    t0 = i * tq
    rows = HPG * tq
    ns = kc_ref.shape[-2]
    nbp = NSA_NBP
    top_n = min(SLC_TOPK, seq // SLC_BLOCK)

    q = q_ref[...]
    qr = jnp.concatenate([q[:, h * HEAD_DIM:(h + 1) * HEAD_DIM] for h in range(HPG)], axis=0)
    qr = (qr * HEAD_DIM ** -0.5).astype(BF16)
    row = lax.broadcasted_iota(jnp.int32, (rows, 1), 0)
    hh = row // tq
    tf = (t0 + row - hh * tq).astype(F32)
    slope = jnp.exp2(-0.5 * (HPG * g + hh + 1).astype(F32))

    kcm = kc_ref[0, 0, 0].astype(BF16)
    vcm = vc_ref[0, 0, 0].astype(BF16)
    jj = lax.broadcasted_iota(jnp.int32, (1, ns), 1)
    dist = tf - (jj * CMP_STRIDE + (CMP_BLOCK - 1)).astype(F32)
    p_cmp = _masked_softmax(_dot_nt(qr, kcm) - slope * dist, dist >= 0.0)
    o_cmp = _dot(p_cmp.astype(BF16), vcm)
    p_sum = p_cmp[0:tq]
    for h in range(1, HPG):
        p_sum = p_sum + p_cmp[h * tq:(h + 1) * tq]
    p_hi, p_lo = _split(p_sum)
    c2s_t = c2s_ref[...]
    imp_t = _dot_nt(c2s_t, p_hi) + _dot_nt(c2s_t, p_lo)

    tt = t0 + lax.broadcasted_iota(jnp.int32, (1, tq), 1)
    cur = tt // SLC_BLOCK
    blk = lax.broadcasted_iota(jnp.int32, (nbp, 1), 0)
    blk_f = blk.astype(F32)
    allowed = blk * SLC_BLOCK <= tt
    forced = jnp.logical_or(blk == 0, jnp.logical_or(blk == cur, blk == cur - 1))
    sc = jnp.where(forced, FORCED_SCORE, jnp.where(allowed, imp_t, NEG_INF))
    sel_t = jnp.zeros((nbp, tq), F32)
    for _ in range(top_n):
        mx = jnp.max(sc, axis=0, keepdims=True)
        idx = jnp.min(jnp.where(sc == mx, blk_f, float(nbp)), axis=0, keepdims=True)
        pick = blk_f == idx
        sel_t = jnp.where(pick, 1.0, sel_t)
        sc = jnp.where(pick, -jnp.inf, sc)
    sel = jnp.where(allowed, sel_t, 0.0).T
    sel_bias = jnp.concatenate([(sel - 1.0) * MASK_BIG] * HPG, axis=0).astype(BF16)

    blk_col = lax.broadcasted_iota(jnp.int32, (nbp, 1), 0)
    key_lane = lax.broadcasted_iota(jnp.int32, (1, kchunk), 1)

    def slc_body(c, carry):
        m, l, acc = carry
        k0 = pl.multiple_of(c * kchunk, kchunk)
        kb = ks_ref[pl.ds(k0, kchunk), :].astype(BF16)
        vb = vs_ref[pl.ds(k0, kchunk), :].astype(BF16)
        pos = k0 + key_lane
        d = tf - pos.astype(F32)
        expand = (blk_col == pos // SLC_BLOCK).astype(BF16)
        s = _dot_nt(qr, kb) + _dot(sel_bias, expand) - jnp.where(d >= 0.0, slope * d, MASK_BIG)
        m_new = jnp.maximum(m, jnp.max(s, axis=-1, keepdims=True))
        alpha = jnp.exp(m - m_new)
        p = jnp.exp(s - m_new)
        l = alpha * l + jnp.sum(p, axis=-1, keepdims=True)
        acc = alpha * acc + _dot(p.astype(BF16), vb)
        return m_new, l, acc

    n_kc = (t0 + tq + kchunk - 1) // kchunk
    _, l_s, acc_s = lax.fori_loop(
        0, n_kc, slc_body,
        (jnp.full((rows, 1), NEG_INF, F32), jnp.zeros((rows, 1), F32), jnp.zeros((rows, HEAD_DIM), F32)))
    o_slc = acc_s / l_s

    wk = WINDOW + tq
    w0 = pl.multiple_of(jnp.maximum(t0 - WINDOW, 0), tq)
    kb = kw_ref[pl.ds(w0, wk), :].astype(BF16)
    vb = vw_ref[pl.ds(w0, wk), :].astype(BF16)
    pos = w0 + lax.broadcasted_iota(jnp.int32, (1, wk), 1)
    d = tf - pos.astype(F32)
    in_window = jnp.abs(d - 0.5 * (WINDOW - 1)) < 0.5 * WINDOW
    s = _dot_nt(qr, kb) - jnp.where(in_window, slope * d, MASK_BIG)
    p_win = jnp.exp(s - jnp.max(s, axis=-1, keepdims=True))
    o_win = _dot(p_win.astype(BF16), vb) / jnp.sum(p_win, axis=-1, keepdims=True)

    gt = _sigmoid(sm_ref[...])
    lane = lax.broadcasted_iota(jnp.int32, (1, SMALL_W), 1)
    branches = (o_cmp, o_slc, o_win)
    for h in range(HPG):
        out_h = jnp.zeros((tq, HEAD_DIM), F32)
        for br in range(3):
            col = SM_NG + br * NSA_HEADS + g * HPG + h
            gate = jnp.sum(jnp.where(lane == col, gt, 0.0), axis=-1, keepdims=True)
            out_h = out_h + gate * branches[br][h * tq:(h + 1) * tq]
        o_ref[:, h * HEAD_DIM:(h + 1) * HEAD_DIM] = out_h.astype(o_ref.dtype)


def _nsa_attend(proj, cmp_kv, c2s, c_small, batch, seq):
    t = proj.shape[0]
    tq = min(NSA_TQ, seq)
    kchunk = min(NSA_KC, seq)
    nq = seq // tq
    ns = cmp_kv.shape[-2]
    assert seq // SLC_BLOCK <= NSA_NBP
    kv_blk = lambda idx: pl.BlockSpec(
        (seq, HEAD_DIM), lambda b, g, i: (b, (C_KV + idx * NSA_KW) // HEAD_DIM + g))
    cmp_blk = lambda kv: pl.BlockSpec((1, 1, 1, ns, HEAD_DIM), lambda b, g, i: (b, kv, g, 0, 0))
    kern = functools.partial(_nsa_kernel, tq=tq, seq=seq, kchunk=kchunk)
    return pl.pallas_call(
        kern,
        grid=(batch, NSA_GROUPS, nq),
        in_specs=[pl.BlockSpec((tq, NSA_KW), lambda b, g, i: (b * nq + i, C_NQ // NSA_KW + g)),
                  cmp_blk(0), cmp_blk(1),
                  kv_blk(2), kv_blk(3), kv_blk(4), kv_blk(5),
                  pl.BlockSpec((tq, SMALL_W), lambda b, g, i: (b * nq + i, c_small // SMALL_W)),
                  pl.BlockSpec((NSA_NBP, ns), lambda b, g, i: (0, 0))],
        out_specs=pl.BlockSpec((tq, NSA_KW), lambda b, g, i: (b * nq + i, g)),
        out_shape=jax.ShapeDtypeStruct((t, NSA_QW), BF16),
        compiler_params=_cparams(("parallel", "parallel", "arbitrary")),
    )(proj, cmp_kv, cmp_kv, proj, proj, proj, proj, proj, c2s)


def _readout_kernel(oa_ref, ob_ref, wa_ref, wb_ref, ma_ref, mb_ref, y_ref):
    ya = _dot(oa_ref[...], wa_ref[...])
    yb = _dot(ob_ref[...], wb_ref[...])
    y_ref[...] = (_sigmoid(ma_ref[...]) * ya + _sigmoid(mb_ref[...]) * yb).astype(y_ref.dtype)


def _readout(o_a, o_b, w_a, w_b, proj, d_model, tm, tn):
    t = o_a.shape[0]
    tm, tn = min(tm, t), min(tn, d_model)
    c_mb = C_MA + d_model
    return pl.pallas_call(
        _readout_kernel,
        grid=(t // tm, d_model // tn),
        in_specs=[pl.BlockSpec((tm, GDN_W), lambda i, j: (i, 0)),
                  pl.BlockSpec((tm, NSA_QW), lambda i, j: (i, 0)),
                  pl.BlockSpec((GDN_W, tn), lambda i, j: (0, j)),
                  pl.BlockSpec((NSA_QW, tn), lambda i, j: (0, j)),
                  pl.BlockSpec((tm, tn), lambda i, j: (i, C_MA // tn + j)),
                  pl.BlockSpec((tm, tn), lambda i, j: (i, c_mb // tn + j))],
        out_specs=pl.BlockSpec((tm, tn), lambda i, j: (i, j)),
        out_shape=jax.ShapeDtypeStruct((t, d_model), BF16),
        compiler_params=_cparams(("parallel", "parallel")),
    )(o_a, o_b, w_a, w_b, proj, proj)


def _cmp_to_slc(seq):
    ns = seq // CMP_STRIDE
    nb = seq // SLC_BLOCK
    start = np.arange(ns) * CMP_STRIDE
    end = start + CMP_BLOCK - 1
    s0 = np.arange(nb) * SLC_BLOCK
    m = (end[:, None] >= s0[None, :]) & (start[:, None] <= s0[None, :] + SLC_BLOCK - 1)
    m[ns - 1] = False
    out = np.zeros((NSA_NBP, ns), np.float32)
    out[:nb] = m.T
    return jnp.asarray(out, dtype=BF16)


def _permute_w_in(w, d_model):
    g0 = 4 * GDN_W
    s0 = g0 + 2 * GDN_HEADS
    s1 = s0 + NSA_QW + 6 * NSA_KW
    s2 = s1 + 3 * NSA_HEADS
    n_main = C_MA + 2 * d_model
    total = -(-(n_main + SMALL_W) // PROJ_TN) * PROJ_TN
    pad = total - n_main - 2 * GDN_HEADS - 3 * NSA_HEADS
    parts = [w[:, :g0], w[:, s0:s1], w[:, s2:], w[:, g0:s0], w[:, s1:s2]]
    parts = [p.astype(BF16) for p in parts] + [jnp.zeros((w.shape[0], pad), BF16)]
    return jnp.concatenate(parts, axis=1)


def _mixer(u, proj_w, conv_w, a_log, dt_bias, norm_w, cmp_pos, cmp_w1, cmp_w2, w_read_a, w_read_b,
           w_out, c2s, batch, seq, d_model):
    c_small = C_MA + 2 * d_model
    proj = _matmul(u, proj_w, tm=1024, tn=PROJ_TN)

    qkv = _gdn_prep(proj, conv_w, seq, ts=512)
    small = proj[:, c_small:c_small + SMALL_W].reshape(batch, seq, SMALL_W)

    def head_rows(cols):
        return jnp.transpose(cols, (0, 2, 1)).reshape(batch * GDN_HEADS, seq // GDN_GROUP, GDN_GROUP)

    pb_t = head_rows(small[:, :, SM_PB:SM_PB + GDN_HEADS])
    pa_t = head_rows(small[:, :, SM_PA:SM_PA + GDN_HEADS])
    bcast = lambda v: jnp.broadcast_to(v.reshape(GDN_HEADS, 1, 1), (GDN_HEADS, 1, HEAD_DIM))
    o_a = _gdn_chunk(qkv, proj, pb_t, pa_t, bcast(a_log), bcast(dt_bias), norm_w.reshape(1, HEAD_DIM),
                     batch, seq)

    ns = seq // CMP_STRIDE

    def segments(c0):
        cols = proj[:, c0:c0 + NSA_KW].reshape(batch, ns, CMP_STRIDE, NSA_GROUPS, HEAD_DIM)
        return jnp.transpose(cols, (0, 3, 1, 2, 4)).reshape(batch, NSA_GROUPS, ns, CMP_STRIDE * HEAD_DIM)

    segs = jnp.stack([segments(C_KV), segments(C_KV + NSA_KW)], axis=1)
    cmp_kv = _nsa_compress(segs,
                           cmp_w1.reshape(2, CMP_BLOCK * HEAD_DIM, HEAD_DIM).astype(BF16),
                           cmp_pos.reshape(2, 1, CMP_BLOCK * HEAD_DIM),
                           cmp_w2.astype(BF16))
    o_b = _nsa_attend(proj, cmp_kv, c2s, c_small, batch, seq)

    y = _readout(o_a, o_b, w_read_a.astype(BF16), w_read_b.astype(BF16), proj, d_model, tm=512, tn=512)
    return _matmul(y, w_out.astype(BF16), tm=1024, tn=512)


def kernel(x, c, ada_w, ada_b, ada_table, w_in, gdn_conv_w, gdn_a_log, gdn_dt_bias, gdn_norm_w,
           cmp_pos, cmp_w1, cmp_w2, w_read_a, w_read_b, w_out, ln1_g, ln1_b,
           mlp_w1, mlp_b1, mlp_w2, mlp_b2, ln2_g, ln2_b):
    batch, seq, d_model = x.shape
    depth = w_in.shape[0]
    d_ff = mlp_w1.shape[-1]
    t = batch * seq
    alpha = (2.0 * depth) ** 0.25
    c2s = _cmp_to_slc(seq)

    mod = _ada_projection(c, ada_w, ada_b)

    x2 = x.reshape(t, d_model)
    u = None
    for l in range(depth):
        m = (mod + ada_table[l].reshape(1, -1)).reshape(batch, N_ADA, 1, d_model)
        shift1, scale1, gate1, shift2, scale2, gate2 = (m[:, j] for j in range(N_ADA))
        if u is None:
            u = _modulate(x2, scale1, shift1, seq, ts=512)
        h = _mixer(u, _permute_w_in(w_in[l], d_model), gdn_conv_w[l], gdn_a_log[l], gdn_dt_bias[l],
                   gdn_norm_w[l], cmp_pos[l], cmp_w1[l], cmp_w2[l], w_read_a[l], w_read_b[l], w_out[l],
                   c2s, batch, seq, d_model)
        x2, u = _deepnorm_ln(x2, h, gate1, ln1_g[l], ln1_b[l], seq, 256, alpha, scale2, shift2)

        tn1 = min(1024, d_ff)
        act = _matmul(u, mlp_w1[l].astype(BF16), tm=1024, tn=tn1, out_dtype=BF16,
                      extras=(mlp_b1[l].reshape(1, d_ff),),
                      extra_specs=(pl.BlockSpec((1, tn1), lambda i, j, k: (0, j)),),
                      epilogue=lambda acc, b: jnp.square(jnp.maximum(acc + b, 0.0)))
        tn2 = min(1024, d_model)
        h = _matmul(act, mlp_w2[l].astype(BF16), tm=1024, tn=tn2, tk=2048,
                    extras=(mlp_b2[l].reshape(1, d_model),),
                    extra_specs=(pl.BlockSpec((1, tn2), lambda i, j, k: (0, j)),),
                    epilogue=lambda acc, b: acc + b)
        if l + 1 < depth:
            m_next = (mod + ada_table[l + 1].reshape(1, -1)).reshape(batch, N_ADA, 1, d_model)
            x2, u = _deepnorm_ln(x2, h, gate2, ln2_g[l], ln2_b[l], seq, 256, alpha,
                                 m_next[:, 1], m_next[:, 0])
        else:
            x2, _ = _deepnorm_ln(x2, h, gate2, ln2_g[l], ln2_b[l], seq, 256, alpha)
    return x2.reshape(batch, seq, d_model)
```

```python
import functools

import numpy as np
import jax
import jax.numpy as jnp
from jax import lax
from jax.experimental import pallas as pl
from jax.experimental.pallas import tpu as pltpu

F32 = jnp.float32
BF16 = jnp.bfloat16

HEAD_DIM = 128
GDN_HEADS = 16
GDN_CONV = 4
GDN_CHUNK = 64
NSA_HEADS = 16
NSA_GROUPS = 4
HPG = NSA_HEADS // NSA_GROUPS
CMP_BLOCK = 32
CMP_STRIDE = 16
SLC_BLOCK = 64
SLC_TOPK = 16
WINDOW = 512
N_ADA = 6
LN_EPS = 1e-5
RMS_EPS = 1e-6
NEG_INF = -1e30
FORCED_SCORE = 1e6

GDN_W = GDN_HEADS * HEAD_DIM
NSA_QW = NSA_HEADS * HEAD_DIM
NSA_KW = NSA_GROUPS * HEAD_DIM

C_QKV = 0
C_Z = 3 * GDN_W
C_NQ = C_Z + GDN_W
C_KV = C_NQ + NSA_QW
C_MA = C_KV + 6 * NSA_KW
SMALL_W = 128
SM_PB, SM_PA, SM_NG = 0, GDN_HEADS, 2 * GDN_HEADS
PROJ_TN = 512

VMEM_LIMIT = 56 * 1024 * 1024


def _cparams(sem):
    return pltpu.CompilerParams(dimension_semantics=sem, vmem_limit_bytes=VMEM_LIMIT)


def _sigmoid(x):
    return 1.0 / (1.0 + jnp.exp(-x))


def _silu(x):
    return x * _sigmoid(x)


def _dot(a, b):
    return jnp.dot(a, b, preferred_element_type=F32)


def _dot_nt(a, b):
    return lax.dot_general(a, b, (((1,), (1,)), ((), ())), preferred_element_type=F32)


def _split(a):
    hi = a.astype(BF16)
    lo = (a - hi.astype(F32)).astype(BF16)
    return hi, lo


def _mm_kernel(*refs, nk, n_extra, epilogue):
    a_ref, b_ref = refs[0], refs[1]
    extra = refs[2:2 + n_extra]
    o_ref = refs[2 + n_extra]
    if nk == 1:
        acc = _dot(a_ref[...], b_ref[...])
        o_ref[...] = epilogue(acc, *[e[...] for e in extra]).astype(o_ref.dtype)
    else:
        acc_ref = refs[3 + n_extra]
        k = pl.program_id(2)

        @pl.when(k == 0)
        def _():
            acc_ref[...] = jnp.zeros_like(acc_ref)

        acc_ref[...] += _dot(a_ref[...], b_ref[...])

        @pl.when(k == nk - 1)
        def _():
            o_ref[...] = epilogue(acc_ref[...], *[e[...] for e in extra]).astype(o_ref.dtype)


def _matmul(a, b, *, tm, tn, tk=None, out_dtype=F32, extras=(), extra_specs=(), epilogue=None):
    m, kdim = a.shape
    n = b.shape[1]
    tm, tn = min(tm, m), min(tn, n)
    tk = kdim if tk is None else min(tk, kdim)
    nk = kdim // tk
    if epilogue is None:
        epilogue = lambda acc: acc
    kern = functools.partial(_mm_kernel, nk=nk, n_extra=len(extras), epilogue=epilogue)
    in_specs = [pl.BlockSpec((tm, tk), lambda i, j, k: (i, k)),
                pl.BlockSpec((tk, tn), lambda i, j, k: (k, j))] + list(extra_specs)
    scratch = [] if nk == 1 else [pltpu.VMEM((tm, tn), F32)]
    return pl.pallas_call(
        kern,
        grid=(m // tm, n // tn, nk),
        in_specs=in_specs,
        out_specs=pl.BlockSpec((tm, tn), lambda i, j, k: (i, j)),
        out_shape=jax.ShapeDtypeStruct((m, n), out_dtype),
        scratch_shapes=scratch,
        compiler_params=_cparams(("parallel", "parallel", "arbitrary")),
    )(a, b, *extras)


ADA_ROWS = 8
ADA_TN = 512


def _ada_kernel(c_ref, w_ref, b_ref, o_ref):
    o_ref[...] = _dot(_silu(c_ref[...]).astype(BF16), w_ref[...].astype(BF16)) + b_ref[...]


def _ada_projection(c, ada_w, ada_b):
    batch, d = c.shape
    n = ada_w.shape[1]
    tn = min(ADA_TN, n)
    assert batch <= ADA_ROWS
    c_pad = jnp.zeros((ADA_ROWS, d), F32).at[:batch].set(c)
    out = pl.pallas_call(
        _ada_kernel,
        grid=(n // tn,),
        in_specs=[pl.BlockSpec((ADA_ROWS, d), lambda j: (0, 0)),
                  pl.BlockSpec((d, tn), lambda j: (0, j)),
                  pl.BlockSpec((1, tn), lambda j: (0, j))],
        out_specs=pl.BlockSpec((ADA_ROWS, tn), lambda j: (0, j)),
        out_shape=jax.ShapeDtypeStruct((ADA_ROWS, n), F32),
        compiler_params=_cparams(("parallel",)),
    )(c_pad, ada_w, ada_b.reshape(1, n))
    return out[:batch]


def _modulate_kernel(x_ref, sc_ref, sh_ref, u_ref):
    u_ref[...] = (x_ref[...] * (1.0 + sc_ref[0]) + sh_ref[0]).astype(u_ref.dtype)


def _modulate(x2, scale, shift, seq, ts):
    t, d = x2.shape
    ts = min(ts, seq)
    per_b = lambda i: ((i * ts) // seq, 0, 0)
    return pl.pallas_call(
        _modulate_kernel,
        grid=(t // ts,),
        in_specs=[pl.BlockSpec((ts, d), lambda i: (i, 0)),
                  pl.BlockSpec((1, 1, d), per_b),
                  pl.BlockSpec((1, 1, d), per_b)],
        out_specs=pl.BlockSpec((ts, d), lambda i: (i, 0)),
        out_shape=jax.ShapeDtypeStruct((t, d), BF16),
        compiler_params=_cparams(("parallel",)),
    )(x2, scale, shift)


def _ln_kernel(*refs, alpha, emit_u):
    if emit_u:
        x_ref, h_ref, gate_ref, g_ref, b_ref, sc_ref, sh_ref, xo_ref, uo_ref = refs
    else:
        x_ref, h_ref, gate_ref, g_ref, b_ref, xo_ref = refs
    v = alpha * x_ref[...] + (1.0 + gate_ref[0]) * h_ref[...]
    mu = jnp.mean(v, axis=-1, keepdims=True)
    vc = v - mu
    var = jnp.mean(vc * vc, axis=-1, keepdims=True)
    xn = vc * lax.rsqrt(var + LN_EPS) * g_ref[...] + b_ref[...]
    xo_ref[...] = xn
    if emit_u:
        uo_ref[...] = (xn * (1.0 + sc_ref[0]) + sh_ref[0]).astype(uo_ref.dtype)


def _deepnorm_ln(x2, h2, gate, ln_g, ln_b, seq, ts, alpha, next_scale=None, next_shift=None):
    t, d = x2.shape
    ts = min(ts, seq)
    emit_u = next_scale is not None
    per_b = lambda i: ((i * ts) // seq, 0, 0)
    row = pl.BlockSpec((ts, d), lambda i: (i, 0))
    vec_b = pl.BlockSpec((1, 1, d), per_b)
    vec = pl.BlockSpec((1, d), lambda i: (0, 0))
    in_specs = [row, row, vec_b, vec, vec]
    args = [x2, h2, gate, ln_g.reshape(1, d), ln_b.reshape(1, d)]
    out_specs = [row]
    out_shape = [jax.ShapeDtypeStruct((t, d), F32)]
    if emit_u:
        in_specs += [vec_b, vec_b]
        args += [next_scale, next_shift]
        out_specs.append(row)
        out_shape.append(jax.ShapeDtypeStruct((t, d), BF16))
    res = pl.pallas_call(
        functools.partial(_ln_kernel, alpha=alpha, emit_u=emit_u),
        grid=(t // ts,),
        in_specs=in_specs,
        out_specs=out_specs,
        out_shape=out_shape,
        compiler_params=_cparams(("parallel",)),
    )(*args)
    return (res[0], res[1]) if emit_u else (res[0], None)


GDN_PREP_TN = 512


def _gdn_prep_kernel(cur_ref, halo_ref, w_ref, o_ref, buf_ref, *, ts, seq):
    i = pl.program_id(0)
    j = pl.program_id(1)
    first = (i * ts) % seq == 0
    buf_ref[0:8, :] = jnp.where(first, 0.0, halo_ref[...])
    buf_ref[8:8 + ts, :] = cur_ref[...]
    w = w_ref[...]
    y = w[3:4] * buf_ref[8:8 + ts, :]
    for tap in range(GDN_CONV - 1):
        y = y + w[tap:tap + 1] * buf_ref[5 + tap:5 + tap + ts, :]
    y = _silu(y)
    heads_per_blk = GDN_PREP_TN // HEAD_DIM
    is_q = j < GDN_W // GDN_PREP_TN
    is_qk = j < 2 * GDN_W // GDN_PREP_TN
    for h in range(heads_per_blk):
        yh = y[:, h * HEAD_DIM:(h + 1) * HEAD_DIM]
        r = lax.rsqrt(jnp.sum(yh * yh, axis=-1, keepdims=True) + RMS_EPS)
        scale = jnp.where(is_q, r * HEAD_DIM ** -0.5, jnp.where(is_qk, r, 1.0))
        o_ref[:, h * HEAD_DIM:(h + 1) * HEAD_DIM] = (yh * scale).astype(o_ref.dtype)


def _gdn_prep(proj, conv_w, seq, ts):
    t = proj.shape[0]
    ts = min(ts, seq)
    tn = GDN_PREP_TN
    kern = functools.partial(_gdn_prep_kernel, ts=ts, seq=seq)
    return pl.pallas_call(
        kern,
        grid=(t // ts, 3 * GDN_W // tn),
        in_specs=[pl.BlockSpec((ts, tn), lambda i, j: (i, j)),
                  pl.BlockSpec((8, tn), lambda i, j: (jnp.maximum(i * (ts // 8) - 1, 0), j)),
                  pl.BlockSpec((GDN_CONV, tn), lambda i, j: (0, j))],
        out_specs=pl.BlockSpec((ts, tn), lambda i, j: (i, j)),
        out_shape=jax.ShapeDtypeStruct((t, 3 * GDN_W), BF16),
        scratch_shapes=[pltpu.VMEM((ts + 8, tn), F32)],
        compiler_params=_cparams(("parallel", "parallel")),
    )(proj, proj, conv_w)


GDN_GROUP = 256
GDN_HB = 2
GDN_NG = 2


def _bmm(a, b):
    return lax.dot_general(a, b, (((2,), (1,)), ((0,), (0,))), preferred_element_type=F32)


def _bmm_nt(a, b):
    return lax.dot_general(a, b, (((2,), (2,)), ((0,), (0,))), preferred_element_type=F32)


def _unit_lower_inverse(a, ri, ci):
    grp = a.shape[-1]
    eye = (ri == ci).astype(F32)
    blk16 = (ri // 16) == (ci // 16)
    blk32 = (ri // 32) == (ci // 32)
    d = jnp.where(blk16, a, 0.0)
    x = eye - d
    d_bf = d.astype(BF16)
    p = _bmm(d_bf, d_bf)
    for _ in range(2):
        p_bf = p.astype(BF16)
        r = _bmm(jnp.concatenate([x.astype(BF16), p_bf], axis=1), p_bf)
        x = x + r[:, :grp]
        p = r[:, grp:]
    x = x + _bmm(x.astype(BF16), p.astype(BF16))
    for e in (jnp.where(blk16, 0.0, jnp.where(blk32, a, 0.0)), jnp.where(blk32, 0.0, a)):
        x_bf = x.astype(BF16)
        x = x - _bmm(_bmm(x_bf, e.astype(BF16)).astype(BF16), x_bf)
    return x


def _gdn_chunk_kernel(q_ref, k_ref, v_ref, z_ref, small_ref, alog_ref, dtb_ref, nw_ref, o_ref, smt_ref,
                      *, n_trips):
    grp, c_len, hb, ng = GDN_GROUP, GDN_CHUNK, GDN_HB, GDN_NG
    cpg = grp // c_len
    span = ng * grp
    ri = lax.broadcasted_iota(jnp.int32, (grp, grp), 0)
    ci = lax.broadcasted_iota(jnp.int32, (grp, grp), 1)
    same = (ri // c_len) == (ci // c_len)
    eye = ri == ci
    incl = jnp.logical_and(same, ri >= ci)
    incl_t = jnp.logical_and(same, ci >= ri)
    col_chunk = lax.broadcasted_iota(jnp.int32, (HEAD_DIM, grp), 1) // c_len
    nw = nw_ref[...]
    h0 = pl.program_id(1) * hb
    a_coef = jnp.stack([-jnp.exp(alog_ref[hh][:, 0:1]) for _ in range(ng) for hh in range(hb)])
    dtb = jnp.stack([dtb_ref[hh][:, 0:1] for _ in range(ng) for hh in range(hb)])

    def body(it, state):
        r0 = pl.multiple_of(it * span, span)

        def load(ref):
            return jnp.stack([ref[pl.ds(r0 + gg * grp, grp), hh * HEAD_DIM:(hh + 1) * HEAD_DIM]
                              for gg in range(ng) for hh in range(hb)])

        for gg in range(ng):
            smt_ref[gg] = small_ref[pl.ds(r0 + gg * grp, grp), :].T

        def load_row(base):
            return jnp.stack([smt_ref[gg, pl.ds(base + h0 + hh, 1), :] for gg in range(ng) for hh in range(hb)])

        q_b, k_b, v_b = load(q_ref), load(k_ref), load(v_ref)
        kf = k_b.astype(F32)
        pa = load_row(SM_PA) + dtb
        softplus = jnp.maximum(pa, 0.0) + jnp.log(1.0 + jnp.exp(-jnp.abs(pa)))
        la_row = a_coef * softplus
        beta_row = _sigmoid(load_row(SM_PB))
        nb = la_row.shape[0]
        la_b = jnp.broadcast_to(la_row, (nb, grp, grp))
        g_col = jnp.sum(jnp.where(incl, la_b, 0.0), axis=2, keepdims=True)
        gl_col = jnp.sum(jnp.where(same, la_b, 0.0), axis=2, keepdims=True)
        la_col = jnp.sum(jnp.where(eye, la_b, 0.0), axis=2, keepdims=True)
        beta_col = jnp.sum(jnp.where(eye, jnp.broadcast_to(beta_row, (nb, grp, grp)), 0.0),
                           axis=2, keepdims=True)
        g_row = jnp.sum(jnp.where(incl_t, jnp.broadcast_to(la_col, (nb, grp, grp)), 0.0),
                        axis=1, keepdims=True)
        decay = jnp.where(incl, jnp.exp(jnp.where(incl, g_col - g_row, 0.0)), 0.0)
        kq = _bmm_nt(jnp.concatenate([k_b, q_b], axis=1), k_b)
        a_mat = jnp.where(eye, 0.0, beta_col * kq[:, :grp] * decay)
        t_inv = _unit_lower_inverse(a_mat, ri, ci).astype(BF16)
        rhs = jnp.concatenate([(beta_col * jnp.exp(g_col)) * kf, beta_col * v_b.astype(F32)], axis=2)
        wu = _bmm(t_inv, rhs.astype(BF16)).astype(BF16)
        qk = (kq[:, grp:] * decay).astype(BF16)
        qk_wu = _bmm(qk, wu)
        q_eff = (q_b.astype(F32) * jnp.exp(g_col) - qk_wu[:, :, :HEAD_DIM]).astype(BF16)
        o_loc = qk_wu[:, :, HEAD_DIM:]
        k_end = (kf * jnp.exp(gl_col - g_col)).astype(BF16)
        k_end_t = jnp.stack([k_end[b].T for b in range(nb)])
        chunk_decay = jnp.exp(gl_col)
        trans = [_bmm(jnp.where(col_chunk == cc, k_end_t, jnp.zeros_like(k_end_t)), wu) for cc in range(cpg)]

        outs = []
        for gg in range(ng):
            ent = slice(gg * hb, (gg + 1) * hb)
            for cc in range(cpg):
                rows = slice(cc * c_len, (cc + 1) * c_len)
                lhs = jnp.concatenate([q_eff[ent, rows], trans[cc][ent, :, :HEAD_DIM].astype(BF16)], axis=1)
                r = _bmm(lhs, state.astype(BF16))
                outs.append(r[:, :c_len] + o_loc[ent, rows])
                state = (chunk_decay[ent, cc * c_len:cc * c_len + 1] * state - r[:, c_len:]
                         + trans[cc][ent, :, HEAD_DIM:])
        o = jnp.concatenate(outs, axis=1)
        o = o * lax.rsqrt(jnp.mean(o * o, axis=-1, keepdims=True) + RMS_EPS) * nw
        o_all = jnp.concatenate([o[hh] for hh in range(hb)], axis=1) * _silu(z_ref[pl.ds(r0, span), :])
        o_ref[pl.ds(r0, span), :] = o_all.astype(o_ref.dtype)
        return state

    lax.fori_loop(0, n_trips, body, jnp.zeros((hb, HEAD_DIM, HEAD_DIM), F32))


def _gdn_chunk(qkv, proj, a_log, dt_bias, norm_w, c_small, batch, seq):
    t = qkv.shape[0]
    n_groups = seq // GDN_GROUP
    assert n_groups % GDN_NG == 0
    width = GDN_HB * HEAD_DIM
    nhb = GDN_HEADS // GDN_HB
    blk = lambda off: pl.BlockSpec((seq, width), lambda b, h: (b, off + h))
    head_vec = pl.BlockSpec((GDN_HB, 1, HEAD_DIM), lambda b, h: (h, 0, 0))
    kern = functools.partial(_gdn_chunk_kernel, n_trips=n_groups // GDN_NG)
    return pl.pallas_call(
        kern,
        grid=(batch, nhb),
        in_specs=[blk(0), blk(nhb), blk(2 * nhb),
                  pl.BlockSpec((seq, width), lambda b, h: (b, C_Z // width + h)),
                  pl.BlockSpec((seq, SMALL_W), lambda b, h: (b, c_small // SMALL_W)),
                  head_vec, head_vec,
                  pl.BlockSpec((1, HEAD_DIM), lambda b, h: (0, 0))],
        out_specs=pl.BlockSpec((seq, width), lambda b, h: (b, h)),
        out_shape=jax.ShapeDtypeStruct((t, GDN_W), BF16),
        scratch_shapes=[pltpu.VMEM((GDN_NG, SMALL_W, GDN_GROUP), F32)],
        compiler_params=_cparams(("parallel", "parallel")),
    )(qkv, qkv, qkv, proj, proj, a_log, dt_bias, norm_w)


def _cmp_kernel(x_ref, w1_ref, pos_ref, w2_ref, o_ref):
    ns = x_ref.shape[0] // CMP_STRIDE
    both = jnp.zeros((ns, 2 * HEAD_DIM), F32)
    pos_term = jnp.zeros((8, HEAD_DIM), F32)
    for r in range(CMP_STRIDE):
        w_r = w1_ref[0, r]
        both = both + _dot(x_ref[pl.ds(r, ns, stride=CMP_STRIDE), :].astype(BF16), w_r)
        for half in range(2):
            p_row = jnp.broadcast_to(pos_ref[0, half * CMP_STRIDE + r:half * CMP_STRIDE + r + 1, :],
                                     (8, HEAD_DIM)).astype(BF16)
            pos_term = pos_term + _dot(p_row, w_r[:, half * HEAD_DIM:(half + 1) * HEAD_DIM])
    first, second = both[:, :HEAD_DIM], both[:, HEAD_DIM:]
    hid = _silu(first + pltpu.roll(second, ns - 1, axis=0) + pos_term[0:1])
    out = _dot(hid.astype(BF16), w2_ref[0])
    rows = lax.broadcasted_iota(jnp.int32, out.shape, 0)
    o_ref[0, 0, 0] = jnp.where(rows < ns - 1, out, 0.0)


def _nsa_compress(proj, w1cat, pos, w2, batch, seq):
    ns = seq // CMP_STRIDE
    g = NSA_GROUPS
    return pl.pallas_call(
        _cmp_kernel,
        grid=(batch, 2, g),
        in_specs=[pl.BlockSpec((seq, HEAD_DIM),
                               lambda bi, kv, gi: (bi, C_KV // HEAD_DIM + kv * NSA_GROUPS + gi)),
                  pl.BlockSpec((1, CMP_STRIDE, HEAD_DIM, 2 * HEAD_DIM), lambda bi, kv, gi: (kv, 0, 0, 0)),
                  pl.BlockSpec((1, CMP_BLOCK, HEAD_DIM), lambda bi, kv, gi: (kv, 0, 0)),
                  pl.BlockSpec((1, HEAD_DIM, HEAD_DIM), lambda bi, kv, gi: (kv, 0, 0))],
        out_specs=pl.BlockSpec((1, 1, 1, ns, HEAD_DIM), lambda bi, kv, gi: (bi, kv, gi, 0, 0)),
        out_shape=jax.ShapeDtypeStruct((batch, 2, g, ns, HEAD_DIM), F32),
        compiler_params=_cparams(("parallel", "parallel", "parallel")),
    )(proj, w1cat, pos, w2)


NSA_TQ = 256
NSA_KC = 512
NSA_NBP = 128
NSA_MAX_BLOCKS = 64
FEAT_POS_HI = 64
FEAT_POS_LO = 67
SLOPE_TERMS = 3
MASK_BIG = 1e30
LOG2E = 1.4426950408889634


def _masked_softmax(s, valid):
    s = jnp.where(valid, s, NEG_INF)
    m = jnp.max(s, axis=-1, keepdims=True)
    p = jnp.where(valid, jnp.exp(s - m), 0.0)
    l = jnp.sum(p, axis=-1, keepdims=True)
    return p / jnp.where(l > 0.0, l, 1.0)


def _key_features(seq):
    pos = np.arange(seq)
    f = np.zeros((seq, HEAD_DIM), np.float32)
    f[pos, pos // SLC_BLOCK] = 1.0
    f[:, FEAT_POS_HI:FEAT_POS_HI + SLOPE_TERMS] = (SLC_BLOCK * (pos // SLC_BLOCK))[:, None]
    f[:, FEAT_POS_LO:FEAT_POS_LO + SLOPE_TERMS] = (pos % SLC_BLOCK)[:, None]
    return jnp.asarray(f, dtype=BF16)


def _nsa_kernel(q_ref, kc_ref, vc_ref, ks_ref, vst_ref, kw_ref, vwt_ref, sm_ref, c2s_ref, feat_ref, o_ref,
                *, tq, seq, kchunk):
    g = pl.program_id(1)
    i = pl.program_id(2)
    t0 = i * tq
    rows = HPG * tq
    ns = kc_ref.shape[-2]
    nbp = NSA_NBP
    top_n = min(SLC_TOPK, seq // SLC_BLOCK)

    q = q_ref[...]
    q32 = jnp.concatenate([q[:, h * HEAD_DIM:(h + 1) * HEAD_DIM] for h in range(HPG)], axis=0)
    q32 = q32 * HEAD_DIM ** -0.5
    qr = q32.astype(BF16)
    qr2 = (q32 * LOG2E).astype(BF16)
    row = lax.broadcasted_iota(jnp.int32, (rows, 1), 0)
    hh = row // tq
    tf = (t0 + row - hh * tq).astype(F32)
    slope = jnp.exp2(-0.5 * (HPG * g + hh + 1).astype(F32))

    kcm = kc_ref[0, 0, 0].astype(BF16)
    vcm = vc_ref[0, 0, 0].astype(BF16)
    jj = lax.broadcasted_iota(jnp.int32, (1, ns), 1)
    dist = tf - (jj * CMP_STRIDE + (CMP_BLOCK - 1)).astype(F32)
    p_cmp = _masked_softmax(_dot_nt(qr, kcm) - slope * dist, dist >= 0.0)
    o_cmp = _dot(p_cmp.astype(BF16), vcm)
    p_sum = p_cmp[0:tq]
    for h in range(1, HPG):
        p_sum = p_sum + p_cmp[h * tq:(h + 1) * tq]
    p_hi, p_lo = _split(p_sum)
    c2s_t = c2s_ref[...]
    imp_t = _dot_nt(c2s_t, p_hi) + _dot_nt(c2s_t, p_lo)

    tt = t0 + lax.broadcasted_iota(jnp.int32, (1, tq), 1)
    cur = tt // SLC_BLOCK
    blk = lax.broadcasted_iota(jnp.int32, (nbp, 1), 0)
    blk_f = blk.astype(F32)
    allowed = blk * SLC_BLOCK <= tt
    forced = jnp.logical_or(blk == 0, jnp.logical_or(blk == cur, blk == cur - 1))
    sc = jnp.where(forced, FORCED_SCORE, jnp.where(allowed, imp_t, NEG_INF))
    sel_t = jnp.zeros((nbp, tq), F32)
    for _ in range(top_n):
        mx = jnp.max(sc, axis=0, keepdims=True)
        idx = jnp.min(jnp.where(sc == mx, blk_f, float(nbp)), axis=0, keepdims=True)
        pick = blk_f == idx
        sel_t = jnp.where(pick, 1.0, sel_t)
        sc = jnp.where(pick, -jnp.inf, sc)
    sel = jnp.where(allowed, sel_t, 0.0).T
    sel_bias = jnp.concatenate([(sel - 1.0) * MASK_BIG] * HPG, axis=0)

    lane = lax.broadcasted_iota(jnp.int32, (1, HEAD_DIM), 1)
    sl2 = slope * LOG2E
    terms = []
    rem = sl2
    for _ in range(SLOPE_TERMS):
        term = rem.astype(BF16).astype(F32)
        terms.append(term)
        rem = rem - term
    slope_feat = jnp.zeros((rows, HEAD_DIM), F32)
    for n, term in enumerate(terms):
        hit = jnp.logical_or(lane == FEAT_POS_HI + n, lane == FEAT_POS_LO + n)
        slope_feat = jnp.where(hit, term, slope_feat)
    q_slc = jnp.concatenate([qr2, jnp.where(lane < NSA_MAX_BLOCKS, sel_bias, slope_feat).astype(BF16)], axis=1)
    q_win = jnp.concatenate([qr2, slope_feat.astype(BF16)], axis=1)

    col = lax.broadcasted_iota(jnp.int32, (1, rows), 1)
    t_col = (t0 + col - (col // tq) * tq).astype(F32)
    key_sub = lax.broadcasted_iota(jnp.int32, (kchunk, 1), 0)

    def slc_chunk(c, carry, causal):
        m, l, acc = carry
        k0 = pl.multiple_of(c * kchunk, kchunk)
        kb = jnp.concatenate([ks_ref[pl.ds(k0, kchunk), :].astype(BF16), feat_ref[pl.ds(k0, kchunk), :]], axis=1)
        s = _dot_nt(kb, q_slc)
        if causal:
            s = jnp.where((k0 + key_sub).astype(F32) <= t_col, s, -MASK_BIG)
        m_new = jnp.maximum(m, jnp.max(s, axis=0, keepdims=True))
        alpha = jnp.exp2(m - m_new)
        p = jnp.exp2(s - m_new)
        l = alpha * l + jnp.sum(p, axis=0, keepdims=True)
        acc = alpha * acc + _dot(vst_ref[0, 0, :, pl.ds(k0, kchunk)], p.astype(BF16))
        return m_new, l, acc

    n_full = t0 // kchunk
    carry = lax.fori_loop(
        0, n_full, lambda c, cr: slc_chunk(c, cr, False),
        (jnp.full((1, rows), NEG_INF, F32), jnp.zeros((1, rows), F32), jnp.zeros((HEAD_DIM, rows), F32)))
    _, l_s, acc_s = slc_chunk(n_full, carry, True)
    o_slc = (acc_s / l_s).T

    wk = WINDOW + tq

    def window_attend(w0, mask_fn):
        kb = jnp.concatenate([kw_ref[pl.ds(w0, wk), :].astype(BF16), feat_ref[pl.ds(w0, wk), :]], axis=1)
        pos = (w0 + lax.broadcasted_iota(jnp.int32, (wk, 1), 0)).astype(F32)
        s = mask_fn(_dot_nt(kb, q_win), pos)
        p = jnp.exp2(s - jnp.max(s, axis=0, keepdims=True))
        o_t = _dot(vwt_ref[0, 0, :, pl.ds(w0, wk)], p.astype(BF16)) / jnp.sum(p, axis=0, keepdims=True)
        return o_t.T

    def banded(s, pos):
        left = jnp.where(pos[:tq] > t_col - float(WINDOW), s[:tq], -MASK_BIG)
        right = jnp.where(pos[wk - tq:] <= t_col, s[wk - tq:], -MASK_BIG)
        return jnp.concatenate([left, s[tq:wk - tq], right], axis=0)

    def head_of_sequence(s, pos):
        return jnp.where(pos <= t_col, s, -MASK_BIG)

    o_win = lax.cond(t0 >= WINDOW,
                     lambda: window_attend(pl.multiple_of(t0 - WINDOW, tq), banded),
                     lambda: window_attend(0, head_of_sequence))

    gt = _sigmoid(sm_ref[...])
    lane_s = lax.broadcasted_iota(jnp.int32, (1, SMALL_W), 1)
    branches = (o_cmp, o_slc, o_win)
    for h in range(HPG):
        out_h = jnp.zeros((tq, HEAD_DIM), F32)
        for br in range(3):
            col = SM_NG + br * NSA_HEADS + g * HPG + h
            gate = jnp.sum(jnp.where(lane_s == col, gt, 0.0), axis=-1, keepdims=True)
            out_h = out_h + gate * branches[br][h * tq:(h + 1) * tq]
        o_ref[:, h * HEAD_DIM:(h + 1) * HEAD_DIM] = out_h.astype(o_ref.dtype)


def _nsa_attend(proj, cmp_kv, v_t, c2s, feats, c_small, batch, seq):
    t = proj.shape[0]
    tq = min(NSA_TQ, seq)
    kchunk = min(NSA_KC, seq)
    nq = seq // tq
    ns = cmp_kv.shape[-2]
    assert seq // SLC_BLOCK <= NSA_MAX_BLOCKS and WINDOW % tq == 0 and kchunk % tq == 0 and WINDOW + tq <= seq
    kv_blk = lambda idx: pl.BlockSpec(
        (seq, HEAD_DIM), lambda b, g, i: (b, (C_KV + idx * NSA_KW) // HEAD_DIM + g))
    vt_blk = lambda which: pl.BlockSpec((1, 1, HEAD_DIM, seq), lambda b, g, i: (which * batch + b, g, 0, 0))
    cmp_blk = lambda kv: pl.BlockSpec((1, 1, 1, ns, HEAD_DIM), lambda b, g, i: (b, kv, g, 0, 0))
    kern = functools.partial(_nsa_kernel, tq=tq, seq=seq, kchunk=kchunk)
    return pl.pallas_call(
        kern,
        grid=(batch, NSA_GROUPS, nq),
        in_specs=[pl.BlockSpec((tq, NSA_KW), lambda b, g, i: (b * nq + i, C_NQ // NSA_KW + g)),
                  cmp_blk(0), cmp_blk(1),
                  kv_blk(2), vt_blk(0), kv_blk(4), vt_blk(1),
                  pl.BlockSpec((tq, SMALL_W), lambda b, g, i: (b * nq + i, c_small // SMALL_W)),
                  pl.BlockSpec((NSA_NBP, ns), lambda b, g, i: (0, 0)),
                  pl.BlockSpec((seq, HEAD_DIM), lambda b, g, i: (0, 0))],
        out_specs=pl.BlockSpec((tq, NSA_KW), lambda b, g, i: (b * nq + i, g)),
        out_shape=jax.ShapeDtypeStruct((t, NSA_QW), BF16),
        compiler_params=_cparams(("parallel", "parallel", "arbitrary")),
    )(proj, cmp_kv, cmp_kv, proj, v_t, proj, v_t, proj, c2s, feats)


def _readout_kernel(oa_ref, ob_ref, wa_ref, wb_ref, ma_ref, mb_ref, y_ref):
    ya = _dot(oa_ref[...], wa_ref[...])
    yb = _dot(ob_ref[...], wb_ref[...])
    y_ref[...] = (_sigmoid(ma_ref[...]) * ya + _sigmoid(mb_ref[...]) * yb).astype(y_ref.dtype)


def _readout(o_a, o_b, w_a, w_b, proj, d_model, tm, tn):
    t = o_a.shape[0]
    tm, tn = min(tm, t), min(tn, d_model)
    c_mb = C_MA + d_model
    return pl.pallas_call(
        _readout_kernel,
        grid=(t // tm, d_model // tn),
        in_specs=[pl.BlockSpec((tm, GDN_W), lambda i, j: (i, 0)),
                  pl.BlockSpec((tm, NSA_QW), lambda i, j: (i, 0)),
                  pl.BlockSpec((GDN_W, tn), lambda i, j: (0, j)),
                  pl.BlockSpec((NSA_QW, tn), lambda i, j: (0, j)),
                  pl.BlockSpec((tm, tn), lambda i, j: (i, C_MA // tn + j)),
                  pl.BlockSpec((tm, tn), lambda i, j: (i, c_mb // tn + j))],
        out_specs=pl.BlockSpec((tm, tn), lambda i, j: (i, j)),
        out_shape=jax.ShapeDtypeStruct((t, d_model), BF16),
        compiler_params=_cparams(("parallel", "parallel")),
    )(o_a, o_b, w_a, w_b, proj, proj)


def _cmp_to_slc(seq):
    ns = seq // CMP_STRIDE
    nb = seq // SLC_BLOCK
    start = np.arange(ns) * CMP_STRIDE
    end = start + CMP_BLOCK - 1
    s0 = np.arange(nb) * SLC_BLOCK
    m = (end[:, None] >= s0[None, :]) & (start[:, None] <= s0[None, :] + SLC_BLOCK - 1)
    m[ns - 1] = False
    out = np.zeros((NSA_NBP, ns), np.float32)
    out[:nb] = m.T
    return jnp.asarray(out, dtype=BF16)


def _permute_w_in(w, d_model):
    g0 = 4 * GDN_W
    s0 = g0 + 2 * GDN_HEADS
    s1 = s0 + NSA_QW + 6 * NSA_KW
    s2 = s1 + 3 * NSA_HEADS
    n_main = C_MA + 2 * d_model
    total = -(-(n_main + SMALL_W) // PROJ_TN) * PROJ_TN
    pad = total - n_main - 2 * GDN_HEADS - 3 * NSA_HEADS
    parts = [w[:, :g0], w[:, s0:s1], w[:, s2:], w[:, g0:s0], w[:, s1:s2]]
    parts = [p.astype(BF16) for p in parts] + [jnp.zeros((w.shape[0], pad), BF16)]
    return jnp.concatenate(parts, axis=1)


def _mixer(u, proj_w, conv_w, a_log, dt_bias, norm_w, cmp_pos, cmp_w1, cmp_w2, w_read_a, w_read_b,
           w_out, c2s, feats, batch, seq, d_model):
    c_small = C_MA + 2 * d_model
    proj = _matmul(u, proj_w, tm=1024, tn=PROJ_TN)

    qkv = _gdn_prep(proj, conv_w, seq, ts=512)
    bcast = lambda v: jnp.broadcast_to(v.reshape(GDN_HEADS, 1, 1), (GDN_HEADS, 1, HEAD_DIM))
    o_a = _gdn_chunk(qkv, proj, bcast(a_log), bcast(dt_bias), norm_w.reshape(1, HEAD_DIM), c_small, batch, seq)

    w1cat = jnp.concatenate([cmp_w1[:, :CMP_STRIDE], cmp_w1[:, CMP_STRIDE:]], axis=-1).astype(BF16)
    cmp_kv = _nsa_compress(proj, w1cat, cmp_pos, cmp_w2.astype(BF16), batch, seq)
    v_cols = jnp.stack([proj[:, C_KV + 3 * NSA_KW:C_KV + 4 * NSA_KW], proj[:, C_KV + 5 * NSA_KW:C_KV + 6 * NSA_KW]])
    v_t = jnp.transpose(v_cols.astype(BF16).reshape(2 * batch, seq, NSA_GROUPS, HEAD_DIM), (0, 2, 3, 1))
    o_b = _nsa_attend(proj, cmp_kv, v_t, c2s, feats, c_small, batch, seq)

    y = _readout(o_a, o_b, w_read_a.astype(BF16), w_read_b.astype(BF16), proj, d_model, tm=512, tn=512)
    return _matmul(y, w_out.astype(BF16), tm=1024, tn=512)


def kernel(x, c, ada_w, ada_b, ada_table, w_in, gdn_conv_w, gdn_a_log, gdn_dt_bias, gdn_norm_w,
           cmp_pos, cmp_w1, cmp_w2, w_read_a, w_read_b, w_out, ln1_g, ln1_b,
           mlp_w1, mlp_b1, mlp_w2, mlp_b2, ln2_g, ln2_b):
    batch, seq, d_model = x.shape
    depth = w_in.shape[0]
    d_ff = mlp_w1.shape[-1]
    t = batch * seq
    alpha = (2.0 * depth) ** 0.25
    c2s = _cmp_to_slc(seq)
    feats = _key_features(seq)

    mod = _ada_projection(c, ada_w, ada_b)

    x2 = x.reshape(t, d_model)
    u = None
    for l in range(depth):
        m = (mod + ada_table[l].reshape(1, -1)).reshape(batch, N_ADA, 1, d_model)
        shift1, scale1, gate1, shift2, scale2, gate2 = (m[:, j] for j in range(N_ADA))
        if u is None:
            u = _modulate(x2, scale1, shift1, seq, ts=512)
        h = _mixer(u, _permute_w_in(w_in[l], d_model), gdn_conv_w[l], gdn_a_log[l], gdn_dt_bias[l],
                   gdn_norm_w[l], cmp_pos[l], cmp_w1[l], cmp_w2[l], w_read_a[l], w_read_b[l], w_out[l],
                   c2s, feats, batch, seq, d_model)
        x2, u = _deepnorm_ln(x2, h, gate1, ln1_g[l], ln1_b[l], seq, 256, alpha, scale2, shift2)

        tn1 = min(1024, d_ff)
        act = _matmul(u, mlp_w1[l].astype(BF16), tm=1024, tn=tn1, out_dtype=BF16,
                      extras=(mlp_b1[l].reshape(1, d_ff),),
                      extra_specs=(pl.BlockSpec((1, tn1), lambda i, j, k: (0, j)),),
                      epilogue=lambda acc, b: jnp.square(jnp.maximum(acc + b, 0.0)))
        tn2 = min(1024, d_model)
        h = _matmul(act, mlp_w2[l].astype(BF16), tm=1024, tn=tn2, tk=4096,
                    extras=(mlp_b2[l].reshape(1, d_model),),
                    extra_specs=(pl.BlockSpec((1, tn2), lambda i, j, k: (0, j)),),
                    epilogue=lambda acc, b: acc + b)
        if l + 1 < depth:
            m_next = (mod + ada_table[l + 1].reshape(1, -1)).reshape(batch, N_ADA, 1, d_model)
            x2, u = _deepnorm_ln(x2, h, gate2, ln2_g[l], ln2_b[l], seq, 256, alpha,
                                 m_next[:, 1], m_next[:, 0])
        else:
            x2, _ = _deepnorm_ln(x2, h, gate2, ln2_g[l], ln2_b[l], seq, 256, alpha)
    return x2.reshape(batch, seq, d_model)
```

```python
import functools

import numpy as np
import jax
import jax.numpy as jnp
from jax import lax
from jax.experimental import pallas as pl
from jax.experimental.pallas import tpu as pltpu

F32 = jnp.float32
BF16 = jnp.bfloat16

HEAD_DIM = 128
GDN_HEADS = 16
GDN_CONV = 4
GDN_CHUNK = 64
NSA_HEADS = 16
NSA_GROUPS = 4
HPG = NSA_HEADS // NSA_GROUPS
CMP_BLOCK = 32
CMP_STRIDE = 16
SLC_BLOCK = 64
SLC_TOPK = 16
WINDOW = 512
N_ADA = 6
LN_EPS = 1e-5
RMS_EPS = 1e-6
NEG_INF = -1e30
FORCED_SCORE = 1e6

GDN_W = GDN_HEADS * HEAD_DIM
NSA_QW = NSA_HEADS * HEAD_DIM
NSA_KW = NSA_GROUPS * HEAD_DIM

C_QKV = 0
C_Z = 3 * GDN_W
C_NQ = C_Z + GDN_W
C_KV = C_NQ + NSA_QW
C_MA = C_KV + 6 * NSA_KW
SMALL_W = 128
SM_PB, SM_PA, SM_NG = 0, GDN_HEADS, 2 * GDN_HEADS
PROJ_TN = 512

VMEM_LIMIT = 56 * 1024 * 1024


def _cparams(sem):
    return pltpu.CompilerParams(dimension_semantics=sem, vmem_limit_bytes=VMEM_LIMIT)


def _sigmoid(x):
    return 1.0 / (1.0 + jnp.exp(-x))


def _silu(x):
    return x * _sigmoid(x)


def _dot(a, b):
    return jnp.dot(a, b, preferred_element_type=F32)


def _dot_nt(a, b):
    return lax.dot_general(a, b, (((1,), (1,)), ((), ())), preferred_element_type=F32)


def _split(a):
    hi = a.astype(BF16)
    lo = (a - hi.astype(F32)).astype(BF16)
    return hi, lo


def _mm_kernel(*refs, nk, n_extra, epilogue):
    a_ref, b_ref = refs[0], refs[1]
    extra = refs[2:2 + n_extra]
    o_ref = refs[2 + n_extra]
    if nk == 1:
        acc = _dot(a_ref[...], b_ref[...])
        o_ref[...] = epilogue(acc, *[e[...] for e in extra]).astype(o_ref.dtype)
    else:
        acc_ref = refs[3 + n_extra]
        k = pl.program_id(2)

        @pl.when(k == 0)
        def _():
            acc_ref[...] = jnp.zeros_like(acc_ref)

        acc_ref[...] += _dot(a_ref[...], b_ref[...])

        @pl.when(k == nk - 1)
        def _():
            o_ref[...] = epilogue(acc_ref[...], *[e[...] for e in extra]).astype(o_ref.dtype)


def _matmul(a, b, layer, *, tm, tn, tk=None, out_dtype=F32, extras=(), extra_specs=(), epilogue=None):
    m, kdim = a.shape
    n = b.shape[2]
    tm, tn = min(tm, m), min(tn, n)
    tk = kdim if tk is None else min(tk, kdim)
    nk = kdim // tk
    if epilogue is None:
        epilogue = lambda acc: acc
    kern = functools.partial(_mm_kernel, nk=nk, n_extra=len(extras), epilogue=epilogue)
    in_specs = [pl.BlockSpec((tm, tk), lambda i, j, k: (i, k)),
                pl.BlockSpec((None, tk, tn), lambda i, j, k: (layer, k, j))] + list(extra_specs)
    scratch = [] if nk == 1 else [pltpu.VMEM((tm, tn), F32)]
    return pl.pallas_call(
        kern,
        grid=(m // tm, n // tn, nk),
        in_specs=in_specs,
        out_specs=pl.BlockSpec((tm, tn), lambda i, j, k: (i, j)),
        out_shape=jax.ShapeDtypeStruct((m, n), out_dtype),
        scratch_shapes=scratch,
        compiler_params=_cparams(("parallel", "parallel", "arbitrary")),
    )(a, b, *extras)


ADA_ROWS = 8
ADA_TK = 128


def _ada_kernel(c_ref, w_ref, b_ref, o_ref):
    @pl.when(pl.program_id(0) == 0)
    def _():
        o_ref[...] = jnp.broadcast_to(b_ref[...], o_ref.shape)

    o_ref[...] += _dot(_silu(c_ref[...]).astype(BF16), w_ref[...].astype(BF16))


def _ada_projection(c, ada_w, ada_b):
    batch, d = c.shape
    n = ada_w.shape[1]
    tk = min(ADA_TK, d)
    assert batch <= ADA_ROWS
    c_pad = jnp.zeros((ADA_ROWS, d), F32).at[:batch].set(c)
    out = pl.pallas_call(
        _ada_kernel,
        grid=(d // tk,),
        in_specs=[pl.BlockSpec((ADA_ROWS, tk), lambda k: (0, k)),
                  pl.BlockSpec((tk, n), lambda k: (k, 0)),
                  pl.BlockSpec((1, n), lambda k: (0, 0))],
        out_specs=pl.BlockSpec((ADA_ROWS, n), lambda k: (0, 0)),
        out_shape=jax.ShapeDtypeStruct((ADA_ROWS, n), F32),
        compiler_params=_cparams(("arbitrary",)),
    )(c_pad, ada_w, ada_b.reshape(1, n))
    return out[:batch]


def _modulate_kernel(x_ref, sc_ref, sh_ref, u_ref):
    u_ref[...] = (x_ref[...] * (1.0 + sc_ref[0]) + sh_ref[0]).astype(u_ref.dtype)


def _modulate(x2, scale, shift, seq, ts):
    t, d = x2.shape
    ts = min(ts, seq)
    per_b = lambda i: ((i * ts) // seq, 0, 0)
    return pl.pallas_call(
        _modulate_kernel,
        grid=(t // ts,),
        in_specs=[pl.BlockSpec((ts, d), lambda i: (i, 0)),
                  pl.BlockSpec((1, 1, d), per_b),
                  pl.BlockSpec((1, 1, d), per_b)],
        out_specs=pl.BlockSpec((ts, d), lambda i: (i, 0)),
        out_shape=jax.ShapeDtypeStruct((t, d), BF16),
        compiler_params=_cparams(("parallel",)),
    )(x2, scale, shift)


def _ln_kernel(*refs, alpha, emit_u):
    if emit_u:
        x_ref, h_ref, gate_ref, g_ref, b_ref, sc_ref, sh_ref, xo_ref, uo_ref = refs
    else:
        x_ref, h_ref, gate_ref, g_ref, b_ref, xo_ref = refs
    v = alpha * x_ref[...] + (1.0 + gate_ref[0]) * h_ref[...]
    mu = jnp.mean(v, axis=-1, keepdims=True)
    vc = v - mu
    var = jnp.mean(vc * vc, axis=-1, keepdims=True)
    xn = vc * lax.rsqrt(var + LN_EPS) * g_ref[...] + b_ref[...]
    xo_ref[...] = xn
    if emit_u:
        uo_ref[...] = (xn * (1.0 + sc_ref[0]) + sh_ref[0]).astype(uo_ref.dtype)


def _deepnorm_ln(x2, h2, gate, ln_g, ln_b, seq, ts, alpha, next_scale=None, next_shift=None):
    t, d = x2.shape
    ts = min(ts, seq)
    emit_u = next_scale is not None
    per_b = lambda i: ((i * ts) // seq, 0, 0)
    row = pl.BlockSpec((ts, d), lambda i: (i, 0))
    vec_b = pl.BlockSpec((1, 1, d), per_b)
    vec = pl.BlockSpec((1, d), lambda i: (0, 0))
    in_specs = [row, row, vec_b, vec, vec]
    args = [x2, h2, gate, ln_g.reshape(1, d), ln_b.reshape(1, d)]
    out_specs = [row]
    out_shape = [jax.ShapeDtypeStruct((t, d), F32)]
    if emit_u:
        in_specs += [vec_b, vec_b]
        args += [next_scale, next_shift]
        out_specs.append(row)
        out_shape.append(jax.ShapeDtypeStruct((t, d), BF16))
    res = pl.pallas_call(
        functools.partial(_ln_kernel, alpha=alpha, emit_u=emit_u),
        grid=(t // ts,),
        in_specs=in_specs,
        out_specs=out_specs,
        out_shape=out_shape,
        compiler_params=_cparams(("parallel",)),
    )(*args)
    return (res[0], res[1]) if emit_u else (res[0], None)


GDN_PREP_TN = 512


def _gdn_prep_kernel(cur_ref, halo_ref, w_ref, o_ref, buf_ref, *, ts, seq):
    i = pl.program_id(0)
    j = pl.program_id(1)
    first = (i * ts) % seq == 0
    buf_ref[0:8, :] = jnp.where(first, 0.0, halo_ref[...])
    buf_ref[8:8 + ts, :] = cur_ref[...]
    w = w_ref[...]
    y = w[3:4] * buf_ref[8:8 + ts, :]
    for tap in range(GDN_CONV - 1):
        y = y + w[tap:tap + 1] * buf_ref[5 + tap:5 + tap + ts, :]
    y = _silu(y)
    heads_per_blk = GDN_PREP_TN // HEAD_DIM
    is_q = j < GDN_W // GDN_PREP_TN
    is_qk = j < 2 * GDN_W // GDN_PREP_TN
    for h in range(heads_per_blk):
        yh = y[:, h * HEAD_DIM:(h + 1) * HEAD_DIM]
        r = lax.rsqrt(jnp.sum(yh * yh, axis=-1, keepdims=True) + RMS_EPS)
        scale = jnp.where(is_q, r * HEAD_DIM ** -0.5, jnp.where(is_qk, r, 1.0))
        o_ref[:, h * HEAD_DIM:(h + 1) * HEAD_DIM] = (yh * scale).astype(o_ref.dtype)


def _gdn_prep(proj, conv_w, seq, ts):
    t = proj.shape[0]
    ts = min(ts, seq)
    tn = GDN_PREP_TN
    kern = functools.partial(_gdn_prep_kernel, ts=ts, seq=seq)
    return pl.pallas_call(
        kern,
        grid=(t // ts, 3 * GDN_W // tn),
        in_specs=[pl.BlockSpec((ts, tn), lambda i, j: (i, j)),
                  pl.BlockSpec((8, tn), lambda i, j: (jnp.maximum(i * (ts // 8) - 1, 0), j)),
                  pl.BlockSpec((GDN_CONV, tn), lambda i, j: (0, j))],
        out_specs=pl.BlockSpec((ts, tn), lambda i, j: (i, j)),
        out_shape=jax.ShapeDtypeStruct((t, 3 * GDN_W), BF16),
        scratch_shapes=[pltpu.VMEM((ts + 8, tn), F32)],
        compiler_params=_cparams(("parallel", "parallel")),
    )(proj, proj, conv_w)


GDN_GROUP = 256
GDN_HB = 2
GDN_NG = 2


def _bmm(a, b):
    return lax.dot_general(a, b, (((2,), (1,)), ((0,), (0,))), preferred_element_type=F32)


def _bmm_nt(a, b):
    return lax.dot_general(a, b, (((2,), (2,)), ((0,), (0,))), preferred_element_type=F32)


def _unit_lower_inverse(a, ri, ci):
    grp = a.shape[-1]
    eye = (ri == ci).astype(F32)
    blk16 = (ri // 16) == (ci // 16)
    blk32 = (ri // 32) == (ci // 32)
    d = jnp.where(blk16, a, 0.0)
    x = eye - d
    d_bf = d.astype(BF16)
    p = _bmm(d_bf, d_bf)
    for _ in range(2):
        p_bf = p.astype(BF16)
        r = _bmm(jnp.concatenate([x.astype(BF16), p_bf], axis=1), p_bf)
        x = x + r[:, :grp]
        p = r[:, grp:]
    x = x + _bmm(x.astype(BF16), p.astype(BF16))
    for e in (jnp.where(blk16, 0.0, jnp.where(blk32, a, 0.0)), jnp.where(blk32, 0.0, a)):
        x_bf = x.astype(BF16)
        x = x - _bmm(_bmm(x_bf, e.astype(BF16)).astype(BF16), x_bf)
    return x


def _gdn_chunk_kernel(q_ref, k_ref, v_ref, z_ref, small_ref, alog_ref, dtb_ref, nw_ref, o_ref, smt_ref,
                      *, n_trips):
    grp, c_len, hb, ng = GDN_GROUP, GDN_CHUNK, GDN_HB, GDN_NG
    cpg = grp // c_len
    span = ng * grp
    ri = lax.broadcasted_iota(jnp.int32, (grp, grp), 0)
    ci = lax.broadcasted_iota(jnp.int32, (grp, grp), 1)
    same = (ri // c_len) == (ci // c_len)
    eye = ri == ci
    incl = jnp.logical_and(same, ri >= ci)
    incl_t = jnp.logical_and(same, ci >= ri)
    col_chunk = lax.broadcasted_iota(jnp.int32, (HEAD_DIM, grp), 1) // c_len
    nw = nw_ref[...]
    h0 = pl.program_id(1) * hb
    a_coef = jnp.stack([-jnp.exp(alog_ref[hh][:, 0:1]) for _ in range(ng) for hh in range(hb)])
    dtb = jnp.stack([dtb_ref[hh][:, 0:1] for _ in range(ng) for hh in range(hb)])

    def body(it, state):
        r0 = pl.multiple_of(it * span, span)

        def load(ref):
            return jnp.stack([ref[pl.ds(r0 + gg * grp, grp), hh * HEAD_DIM:(hh + 1) * HEAD_DIM]
                              for gg in range(ng) for hh in range(hb)])

        for gg in range(ng):
            smt_ref[gg] = small_ref[pl.ds(r0 + gg * grp, grp), :].T

        def load_row(base):
            return jnp.stack([smt_ref[gg, pl.ds(base + h0 + hh, 1), :] for gg in range(ng) for hh in range(hb)])

        q_b, k_b, v_b = load(q_ref), load(k_ref), load(v_ref)
        kf = k_b.astype(F32)
        pa = load_row(SM_PA) + dtb
        softplus = jnp.maximum(pa, 0.0) + jnp.log(1.0 + jnp.exp(-jnp.abs(pa)))
        la_row = a_coef * softplus
        beta_row = _sigmoid(load_row(SM_PB))
        nb = la_row.shape[0]
        la_b = jnp.broadcast_to(la_row, (nb, grp, grp))
        g_col = jnp.sum(jnp.where(incl, la_b, 0.0), axis=2, keepdims=True)
        gl_col = jnp.sum(jnp.where(same, la_b, 0.0), axis=2, keepdims=True)
        la_col = jnp.sum(jnp.where(eye, la_b, 0.0), axis=2, keepdims=True)
        beta_col = jnp.sum(jnp.where(eye, jnp.broadcast_to(beta_row, (nb, grp, grp)), 0.0),
                           axis=2, keepdims=True)
        g_row = jnp.sum(jnp.where(incl_t, jnp.broadcast_to(la_col, (nb, grp, grp)), 0.0),
                        axis=1, keepdims=True)
        decay = jnp.where(incl, jnp.exp(jnp.where(incl, g_col - g_row, 0.0)), 0.0)
        kq = _bmm_nt(jnp.concatenate([k_b, q_b], axis=1), k_b)
        a_mat = jnp.where(eye, 0.0, beta_col * kq[:, :grp] * decay)
        t_inv = _unit_lower_inverse(a_mat, ri, ci).astype(BF16)
        rhs = jnp.concatenate([(beta_col * jnp.exp(g_col)) * kf, beta_col * v_b.astype(F32)], axis=2)
        wu = _bmm(t_inv, rhs.astype(BF16)).astype(BF16)
        qk = (kq[:, grp:] * decay).astype(BF16)
        qk_wu = _bmm(qk, wu)
        q_eff = (q_b.astype(F32) * jnp.exp(g_col) - qk_wu[:, :, :HEAD_DIM]).astype(BF16)
        o_loc = qk_wu[:, :, HEAD_DIM:]
        k_end = (kf * jnp.exp(gl_col - g_col)).astype(BF16)
        k_end_t = jnp.stack([k_end[b].T for b in range(nb)])
        chunk_decay = jnp.exp(gl_col)
        trans = [_bmm(jnp.where(col_chunk == cc, k_end_t, jnp.zeros_like(k_end_t)), wu) for cc in range(cpg)]

        outs = []
        for gg in range(ng):
            ent = slice(gg * hb, (gg + 1) * hb)
            for cc in range(cpg):
                rows = slice(cc * c_len, (cc + 1) * c_len)
                lhs = jnp.concatenate([q_eff[ent, rows], trans[cc][ent, :, :HEAD_DIM].astype(BF16)], axis=1)
                r = _bmm(lhs, state.astype(BF16))
                outs.append(r[:, :c_len] + o_loc[ent, rows])
                state = (chunk_decay[ent, cc * c_len:cc * c_len + 1] * state - r[:, c_len:]
                         + trans[cc][ent, :, HEAD_DIM:])
        o = jnp.concatenate(outs, axis=1)
        o = o * lax.rsqrt(jnp.mean(o * o, axis=-1, keepdims=True) + RMS_EPS) * nw
        o_all = jnp.concatenate([o[hh] for hh in range(hb)], axis=1) * _silu(z_ref[pl.ds(r0, span), :])
        o_ref[pl.ds(r0, span), :] = o_all.astype(o_ref.dtype)
        return state

    lax.fori_loop(0, n_trips, body, jnp.zeros((hb, HEAD_DIM, HEAD_DIM), F32))


def _gdn_chunk(qkv, proj, a_log, dt_bias, norm_w, c_small, batch, seq):
    t = qkv.shape[0]
    n_groups = seq // GDN_GROUP
    assert n_groups % GDN_NG == 0
    width = GDN_HB * HEAD_DIM
    nhb = GDN_HEADS // GDN_HB
    blk = lambda off: pl.BlockSpec((seq, width), lambda b, h: (b, off + h))
    head_vec = pl.BlockSpec((GDN_HB, 1, HEAD_DIM), lambda b, h: (h, 0, 0))
    kern = functools.partial(_gdn_chunk_kernel, n_trips=n_groups // GDN_NG)
    return pl.pallas_call(
        kern,
        grid=(batch, nhb),
        in_specs=[blk(0), blk(nhb), blk(2 * nhb),
                  pl.BlockSpec((seq, width), lambda b, h: (b, C_Z // width + h)),
                  pl.BlockSpec((seq, SMALL_W), lambda b, h: (b, c_small // SMALL_W)),
                  head_vec, head_vec,
                  pl.BlockSpec((1, HEAD_DIM), lambda b, h: (0, 0))],
        out_specs=pl.BlockSpec((seq, width), lambda b, h: (b, h)),
        out_shape=jax.ShapeDtypeStruct((t, GDN_W), BF16),
        scratch_shapes=[pltpu.VMEM((GDN_NG, SMALL_W, GDN_GROUP), F32)],
        compiler_params=_cparams(("parallel", "parallel")),
    )(qkv, qkv, qkv, proj, proj, a_log, dt_bias, norm_w)


def _cmp_kernel(x_ref, w1_ref, pos_ref, w2_ref, o_ref):
    ns = x_ref.shape[0] // CMP_STRIDE
    both = jnp.zeros((ns, 2 * HEAD_DIM), F32)
    pos_term = jnp.zeros((8, HEAD_DIM), F32)
    for r in range(CMP_STRIDE):
        w_r = w1_ref[0, r]
        both = both + _dot(x_ref[pl.ds(r, ns, stride=CMP_STRIDE), :].astype(BF16), w_r)
        for half in range(2):
            p_row = jnp.broadcast_to(pos_ref[0, half * CMP_STRIDE + r:half * CMP_STRIDE + r + 1, :],
                                     (8, HEAD_DIM)).astype(BF16)
            pos_term = pos_term + _dot(p_row, w_r[:, half * HEAD_DIM:(half + 1) * HEAD_DIM])
    first, second = both[:, :HEAD_DIM], both[:, HEAD_DIM:]
    hid = _silu(first + pltpu.roll(second, ns - 1, axis=0) + pos_term[0:1])
    out = _dot(hid.astype(BF16), w2_ref[0])
    rows = lax.broadcasted_iota(jnp.int32, out.shape, 0)
    o_ref[0, 0, 0] = jnp.where(rows < ns - 1, out, 0.0)


def _nsa_compress(proj, w1cat, pos, w2, batch, seq):
    ns = seq // CMP_STRIDE
    g = NSA_GROUPS
    return pl.pallas_call(
        _cmp_kernel,
        grid=(batch, 2, g),
        in_specs=[pl.BlockSpec((seq, HEAD_DIM),
                               lambda bi, kv, gi: (bi, C_KV // HEAD_DIM + kv * NSA_GROUPS + gi)),
                  pl.BlockSpec((1, CMP_STRIDE, HEAD_DIM, 2 * HEAD_DIM), lambda bi, kv, gi: (kv, 0, 0, 0)),
                  pl.BlockSpec((1, CMP_BLOCK, HEAD_DIM), lambda bi, kv, gi: (kv, 0, 0)),
                  pl.BlockSpec((1, HEAD_DIM, HEAD_DIM), lambda bi, kv, gi: (kv, 0, 0))],
        out_specs=pl.BlockSpec((1, 1, 1, ns, HEAD_DIM), lambda bi, kv, gi: (bi, kv, gi, 0, 0)),
        out_shape=jax.ShapeDtypeStruct((batch, 2, g, ns, HEAD_DIM), F32),
        compiler_params=_cparams(("parallel", "parallel", "parallel")),
    )(proj, w1cat, pos, w2)


NSA_TQ = 256
NSA_KC = 512
NSA_NBP = 128
NSA_MAX_BLOCKS = 64
FEAT_POS_HI = 64
FEAT_POS_LO = 67
SLOPE_TERMS = 3
MASK_BIG = 1e30
LOG2E = 1.4426950408889634


def _masked_softmax(s, valid):
    s = jnp.where(valid, s, NEG_INF)
    m = jnp.max(s, axis=-1, keepdims=True)
    p = jnp.where(valid, jnp.exp(s - m), 0.0)
    l = jnp.sum(p, axis=-1, keepdims=True)
    return p / jnp.where(l > 0.0, l, 1.0)


def _key_features(seq):
    pos = np.arange(seq)
    f = np.zeros((seq, HEAD_DIM), np.float32)
    f[pos, pos // SLC_BLOCK] = 1.0
    f[:, FEAT_POS_HI:FEAT_POS_HI + SLOPE_TERMS] = (SLC_BLOCK * (pos // SLC_BLOCK))[:, None]
    f[:, FEAT_POS_LO:FEAT_POS_LO + SLOPE_TERMS] = (pos % SLC_BLOCK)[:, None]
    return jnp.asarray(f, dtype=BF16)


def _nsa_kernel(q_ref, kc_ref, vc_ref, ks_ref, vs_ref, kw_ref, vw_ref, sm_ref, c2s_ref, feat_ref, o_ref,
                *, tq, seq, kchunk):
    g = pl.program_id(1)
    i = pl.program_id(2)
    t0 = i * tq
    rows = HPG * tq
    ns = kc_ref.shape[-2]
    nbp = NSA_NBP
    top_n = min(SLC_TOPK, seq // SLC_BLOCK)

    q = q_ref[...]
    q32 = jnp.concatenate([q[:, h * HEAD_DIM:(h + 1) * HEAD_DIM] for h in range(HPG)], axis=0)
    q32 = q32 * HEAD_DIM ** -0.5
    qr = q32.astype(BF16)
    qr2 = (q32 * LOG2E).astype(BF16)
    row = lax.broadcasted_iota(jnp.int32, (rows, 1), 0)
    hh = row // tq
    tf = (t0 + row - hh * tq).astype(F32)
    slope = jnp.exp2(-0.5 * (HPG * g + hh + 1).astype(F32))

    kcm = kc_ref[0, 0, 0].astype(BF16)
    vcm = vc_ref[0, 0, 0].astype(BF16)
    jj = lax.broadcasted_iota(jnp.int32, (1, ns), 1)
    dist = tf - (jj * CMP_STRIDE + (CMP_BLOCK - 1)).astype(F32)
    p_cmp = _masked_softmax(_dot_nt(qr, kcm) - slope * dist, dist >= 0.0)
    o_cmp = _dot(p_cmp.astype(BF16), vcm)
    p_sum = p_cmp[0:tq]
    for h in range(1, HPG):
        p_sum = p_sum + p_cmp[h * tq:(h + 1) * tq]
    p_hi, p_lo = _split(p_sum)
    c2s_t = c2s_ref[...]
    imp_t = _dot_nt(c2s_t, p_hi) + _dot_nt(c2s_t, p_lo)

    nbr = NSA_MAX_BLOCKS
    tt = t0 + lax.broadcasted_iota(jnp.int32, (1, tq), 1)
    cur = tt // SLC_BLOCK
    blk = lax.broadcasted_iota(jnp.int32, (nbr, 1), 0)
    blk_f = blk.astype(F32)
    allowed = blk * SLC_BLOCK <= tt
    forced = jnp.logical_or(blk == 0, jnp.logical_or(blk == cur, blk == cur - 1))
    sc = jnp.where(forced, FORCED_SCORE, jnp.where(allowed, imp_t[:nbr], NEG_INF))
    sel_t = jnp.zeros((nbr, tq), F32)
    for _ in range(top_n):
        mx = jnp.max(sc, axis=0, keepdims=True)
        idx = jnp.min(jnp.where(sc == mx, blk_f, float(nbr)), axis=0, keepdims=True)
        pick = blk_f == idx
        sel_t = jnp.where(pick, 1.0, sel_t)
        sc = jnp.where(pick, -jnp.inf, sc)
    sel_t = jnp.where(allowed, sel_t, 0.0)
    sel = jnp.concatenate([sel_t, jnp.zeros((nbp - nbr, tq), F32)], axis=0).T
    sel_bias = jnp.concatenate([(sel - 1.0) * MASK_BIG] * HPG, axis=0)

    lane = lax.broadcasted_iota(jnp.int32, (1, HEAD_DIM), 1)
    sl2 = slope * LOG2E
    terms = []
    rem = sl2
    for _ in range(SLOPE_TERMS):
        term = rem.astype(BF16).astype(F32)
        terms.append(term)
        rem = rem - term
    slope_feat = jnp.zeros((rows, HEAD_DIM), F32)
    for n, term in enumerate(terms):
        hit = jnp.logical_or(lane == FEAT_POS_HI + n, lane == FEAT_POS_LO + n)
        slope_feat = jnp.where(hit, term, slope_feat)
    q_slc = jnp.concatenate([qr2, jnp.where(lane < NSA_MAX_BLOCKS, sel_bias, slope_feat).astype(BF16)], axis=1)
    q_win = jnp.concatenate([qr2, slope_feat.astype(BF16)], axis=1)

    col = lax.broadcasted_iota(jnp.int32, (1, rows), 1)
    t_col = (t0 + col - (col // tq) * tq).astype(F32)
    key_sub = lax.broadcasted_iota(jnp.int32, (kchunk, 1), 0)

    def slc_chunk(c, carry, causal):
        m, l, acc = carry
        k0 = pl.multiple_of(c * kchunk, kchunk)
        kb = jnp.concatenate([ks_ref[pl.ds(k0, kchunk), :].astype(BF16), feat_ref[pl.ds(k0, kchunk), :]], axis=1)
        s = _dot_nt(kb, q_slc)
        if causal:
            s = jnp.where((k0 + key_sub).astype(F32) <= t_col, s, -MASK_BIG)
        m_new = jnp.maximum(m, jnp.max(s, axis=0, keepdims=True))
        alpha = jnp.exp2(m - m_new)
        p = jnp.exp2(s - m_new)
        l = alpha * l + jnp.sum(p, axis=0, keepdims=True)
        v_t = vs_ref[pl.ds(k0, kchunk), :].T.astype(BF16)
        acc = alpha * acc + _dot(v_t, p.astype(BF16))
        return m_new, l, acc

    n_full = t0 // kchunk
    carry = lax.fori_loop(
        0, n_full, lambda c, cr: slc_chunk(c, cr, False),
        (jnp.full((1, rows), NEG_INF, F32), jnp.zeros((1, rows), F32), jnp.zeros((HEAD_DIM, rows), F32)))
    _, l_s, acc_s = slc_chunk(n_full, carry, True)
    o_slc = (acc_s / l_s).T

    wk = WINDOW + tq

    def window_attend(w0, mask_fn):
        kb = jnp.concatenate([kw_ref[pl.ds(w0, wk), :].astype(BF16), feat_ref[pl.ds(w0, wk), :]], axis=1)
        pos = (w0 + lax.broadcasted_iota(jnp.int32, (wk, 1), 0)).astype(F32)
        s = mask_fn(_dot_nt(kb, q_win), pos)
        p = jnp.exp2(s - jnp.max(s, axis=0, keepdims=True))
        v_t = vw_ref[pl.ds(w0, wk), :].T.astype(BF16)
        o_t = _dot(v_t, p.astype(BF16)) / jnp.sum(p, axis=0, keepdims=True)
        return o_t.T

    def banded(s, pos):
        left = jnp.where(pos[:tq] > t_col - float(WINDOW), s[:tq], -MASK_BIG)
        right = jnp.where(pos[wk - tq:] <= t_col, s[wk - tq:], -MASK_BIG)
        return jnp.concatenate([left, s[tq:wk - tq], right], axis=0)

    def head_of_sequence(s, pos):
        return jnp.where(pos <= t_col, s, -MASK_BIG)

    o_win = lax.cond(t0 >= WINDOW,
                     lambda: window_attend(pl.multiple_of(t0 - WINDOW, tq), banded),
                     lambda: window_attend(0, head_of_sequence))

    gt = _sigmoid(sm_ref[...])
    lane_s = lax.broadcasted_iota(jnp.int32, (1, SMALL_W), 1)
    branches = (o_cmp, o_slc, o_win)
    for h in range(HPG):
        out_h = jnp.zeros((tq, HEAD_DIM), F32)
        for br in range(3):
            col = SM_NG + br * NSA_HEADS + g * HPG + h
            gate = jnp.sum(jnp.where(lane_s == col, gt, 0.0), axis=-1, keepdims=True)
            out_h = out_h + gate * branches[br][h * tq:(h + 1) * tq]
        o_ref[:, h * HEAD_DIM:(h + 1) * HEAD_DIM] = out_h.astype(o_ref.dtype)


def _nsa_attend(proj, cmp_kv, c2s, feats, c_small, batch, seq):
    t = proj.shape[0]
    tq = min(NSA_TQ, seq)
    kchunk = min(NSA_KC, seq)
    nq = seq // tq
    ns = cmp_kv.shape[-2]
    assert seq // SLC_BLOCK <= NSA_MAX_BLOCKS and WINDOW % tq == 0 and kchunk % tq == 0 and WINDOW + tq <= seq
    kv_blk = lambda idx: pl.BlockSpec(
        (seq, HEAD_DIM), lambda b, g, i: (b, (C_KV + idx * NSA_KW) // HEAD_DIM + g))
    cmp_blk = lambda kv: pl.BlockSpec((1, 1, 1, ns, HEAD_DIM), lambda b, g, i: (b, kv, g, 0, 0))
    kern = functools.partial(_nsa_kernel, tq=tq, seq=seq, kchunk=kchunk)
    return pl.pallas_call(
        kern,
        grid=(batch, NSA_GROUPS, nq),
        in_specs=[pl.BlockSpec((tq, NSA_KW), lambda b, g, i: (b * nq + i, C_NQ // NSA_KW + g)),
                  cmp_blk(0), cmp_blk(1),
                  kv_blk(2), kv_blk(3), kv_blk(4), kv_blk(5),
                  pl.BlockSpec((tq, SMALL_W), lambda b, g, i: (b * nq + i, c_small // SMALL_W)),
                  pl.BlockSpec((NSA_NBP, ns), lambda b, g, i: (0, 0)),
                  pl.BlockSpec((seq, HEAD_DIM), lambda b, g, i: (0, 0))],
        out_specs=pl.BlockSpec((tq, NSA_KW), lambda b, g, i: (b * nq + i, g)),
        out_shape=jax.ShapeDtypeStruct((t, NSA_QW), BF16),
        compiler_params=_cparams(("parallel", "parallel", "arbitrary")),
    )(proj, cmp_kv, cmp_kv, proj, proj, proj, proj, proj, c2s, feats)


def _readout_kernel(oa_ref, ob_ref, wa_ref, wb_ref, ma_ref, mb_ref, y_ref):
    ya = _dot(oa_ref[...], wa_ref[...])
    yb = _dot(ob_ref[...], wb_ref[...])
    y_ref[...] = (_sigmoid(ma_ref[...]) * ya + _sigmoid(mb_ref[...]) * yb).astype(y_ref.dtype)


def _readout(o_a, o_b, w_a, w_b, layer, proj, d_model, tm, tn):
    t = o_a.shape[0]
    tm, tn = min(tm, t), min(tn, d_model)
    c_mb = C_MA + d_model
    return pl.pallas_call(
        _readout_kernel,
        grid=(t // tm, d_model // tn),
        in_specs=[pl.BlockSpec((tm, GDN_W), lambda i, j: (i, 0)),
                  pl.BlockSpec((tm, NSA_QW), lambda i, j: (i, 0)),
                  pl.BlockSpec((None, GDN_W, tn), lambda i, j: (layer, 0, j)),
                  pl.BlockSpec((None, NSA_QW, tn), lambda i, j: (layer, 0, j)),
                  pl.BlockSpec((tm, tn), lambda i, j: (i, C_MA // tn + j)),
                  pl.BlockSpec((tm, tn), lambda i, j: (i, c_mb // tn + j))],
        out_specs=pl.BlockSpec((tm, tn), lambda i, j: (i, j)),
        out_shape=jax.ShapeDtypeStruct((t, d_model), BF16),
        compiler_params=_cparams(("parallel", "parallel")),
    )(o_a, o_b, w_a, w_b, proj, proj)


def _cmp_to_slc(seq):
    ns = seq // CMP_STRIDE
    nb = seq // SLC_BLOCK
    start = np.arange(ns) * CMP_STRIDE
    end = start + CMP_BLOCK - 1
    s0 = np.arange(nb) * SLC_BLOCK
    m = (end[:, None] >= s0[None, :]) & (start[:, None] <= s0[None, :] + SLC_BLOCK - 1)
    m[ns - 1] = False
    out = np.zeros((NSA_NBP, ns), np.float32)
    out[:nb] = m.T
    return jnp.asarray(out, dtype=BF16)


def _permute_w_in(w, d_model):
    g0 = 4 * GDN_W
    s0 = g0 + 2 * GDN_HEADS
    s1 = s0 + NSA_QW + 6 * NSA_KW
    s2 = s1 + 3 * NSA_HEADS
    n_main = C_MA + 2 * d_model
    total = -(-(n_main + SMALL_W) // PROJ_TN) * PROJ_TN
    pad = total - n_main - 2 * GDN_HEADS - 3 * NSA_HEADS
    parts = [w[..., :g0], w[..., s0:s1], w[..., s2:], w[..., g0:s0], w[..., s1:s2]]
    parts = [p.astype(BF16) for p in parts] + [jnp.zeros(w.shape[:-1] + (pad,), BF16)]
    return jnp.concatenate(parts, axis=-1)


def _mixer(u, layer, proj_w, conv_w, a_log, dt_bias, norm_w, cmp_pos, cmp_w1, cmp_w2, w_read_a, w_read_b,
           w_out, c2s, feats, batch, seq, d_model):
    c_small = C_MA + 2 * d_model
    proj = _matmul(u, proj_w, layer, tm=1024, tn=PROJ_TN)

    qkv = _gdn_prep(proj, conv_w, seq, ts=512)
    bcast = lambda v: jnp.broadcast_to(v.reshape(GDN_HEADS, 1, 1), (GDN_HEADS, 1, HEAD_DIM))
    o_a = _gdn_chunk(qkv, proj, bcast(a_log), bcast(dt_bias), norm_w.reshape(1, HEAD_DIM), c_small, batch, seq)

    w1cat = jnp.concatenate([cmp_w1[:, :CMP_STRIDE], cmp_w1[:, CMP_STRIDE:]], axis=-1).astype(BF16)
    cmp_kv = _nsa_compress(proj, w1cat, cmp_pos, cmp_w2.astype(BF16), batch, seq)
    o_b = _nsa_attend(proj, cmp_kv, c2s, feats, c_small, batch, seq)

    y = _readout(o_a, o_b, w_read_a, w_read_b, layer, proj, d_model, tm=512, tn=512)
    return _matmul(y, w_out, layer, tm=1024, tn=512)


def kernel(x, c, ada_w, ada_b, ada_table, w_in, gdn_conv_w, gdn_a_log, gdn_dt_bias, gdn_norm_w,
           cmp_pos, cmp_w1, cmp_w2, w_read_a, w_read_b, w_out, ln1_g, ln1_b,
           mlp_w1, mlp_b1, mlp_w2, mlp_b2, ln2_g, ln2_b):
    batch, seq, d_model = x.shape
    depth = w_in.shape[0]
    d_ff = mlp_w1.shape[-1]
    t = batch * seq
    alpha = (2.0 * depth) ** 0.25
    c2s = _cmp_to_slc(seq)
    feats = _key_features(seq)

    proj_w = _permute_w_in(w_in, d_model)
    w_read_a, w_read_b, w_out = (w.astype(BF16) for w in (w_read_a, w_read_b, w_out))
    mlp_w1, mlp_w2 = mlp_w1.astype(BF16), mlp_w2.astype(BF16)

    mod = _ada_projection(c, ada_w, ada_b)

    x2 = x.reshape(t, d_model)
    u = None
    for l in range(depth):
        m = (mod + ada_table[l].reshape(1, -1)).reshape(batch, N_ADA, 1, d_model)
        shift1, scale1, gate1, shift2, scale2, gate2 = (m[:, j] for j in range(N_ADA))
        if u is None:
            u = _modulate(x2, scale1, shift1, seq, ts=512)
        h = _mixer(u, l, proj_w, gdn_conv_w[l], gdn_a_log[l], gdn_dt_bias[l],
                   gdn_norm_w[l], cmp_pos[l], cmp_w1[l], cmp_w2[l], w_read_a, w_read_b, w_out,
                   c2s, feats, batch, seq, d_model)
        x2, u = _deepnorm_ln(x2, h, gate1, ln1_g[l], ln1_b[l], seq, 256, alpha, scale2, shift2)

        tn1 = min(1024, d_ff)
        act = _matmul(u, mlp_w1, l, tm=1024, tn=tn1, out_dtype=BF16,
                      extras=(mlp_b1[l].reshape(1, d_ff),),
                      extra_specs=(pl.BlockSpec((1, tn1), lambda i, j, k: (0, j)),),
                      epilogue=lambda acc, b: jnp.square(jnp.maximum(acc + b, 0.0)))
        tn2 = min(1024, d_model)
        h = _matmul(act, mlp_w2, l, tm=1024, tn=tn2, tk=4096,
                    extras=(mlp_b2[l].reshape(1, d_model),),
                    extra_specs=(pl.BlockSpec((1, tn2), lambda i, j, k: (0, j)),),
                    epilogue=lambda acc, b: acc + b)
        if l + 1 < depth:
            m_next = (mod + ada_table[l + 1].reshape(1, -1)).reshape(batch, N_ADA, 1, d_model)
            x2, u = _deepnorm_ln(x2, h, gate2, ln2_g[l], ln2_b[l], seq, 256, alpha,
                                 m_next[:, 1], m_next[:, 0])
        else:
            x2, _ = _deepnorm_ln(x2, h, gate2, ln2_g[l], ln2_b[l], seq, 256, alpha)
    return x2.reshape(batch, seq, d_model)
```

```python
import functools

import numpy as np
import jax
import jax.numpy as jnp
from jax import lax
from jax.experimental import pallas as pl
from jax.experimental.pallas import tpu as pltpu

F32 = jnp.float32
BF16 = jnp.bfloat16

HEAD_DIM = 128
GDN_HEADS = 16
GDN_CONV = 4
GDN_CHUNK = 64
NSA_HEADS = 16
NSA_GROUPS = 4
HPG = NSA_HEADS // NSA_GROUPS
CMP_BLOCK = 32
CMP_STRIDE = 16
SLC_BLOCK = 64
SLC_TOPK = 16
WINDOW = 512
N_ADA = 6
LN_EPS = 1e-5
RMS_EPS = 1e-6
NEG_INF = -1e30
FORCED_SCORE = 1e6

GDN_W = GDN_HEADS * HEAD_DIM
NSA_QW = NSA_HEADS * HEAD_DIM
NSA_KW = NSA_GROUPS * HEAD_DIM

C_QKV = 0
C_Z = 3 * GDN_W
C_NQ = C_Z + GDN_W
C_KV = C_NQ + NSA_QW
C_MA = C_KV + 6 * NSA_KW
SMALL_W = 128
SM_PB, SM_PA, SM_NG = 0, GDN_HEADS, 2 * GDN_HEADS
PROJ_TN = 512

VMEM_LIMIT = 56 * 1024 * 1024


def _cparams(sem):
    return pltpu.CompilerParams(dimension_semantics=sem, vmem_limit_bytes=VMEM_LIMIT)


def _sigmoid(x):
    return 1.0 / (1.0 + jnp.exp(-x))


def _silu(x):
    return x * _sigmoid(x)


def _dot(a, b):
    return jnp.dot(a, b, preferred_element_type=F32)


def _dot_nt(a, b):
    return lax.dot_general(a, b, (((1,), (1,)), ((), ())), preferred_element_type=F32)


def _split(a):
    hi = a.astype(BF16)
    lo = (a - hi.astype(F32)).astype(BF16)
    return hi, lo


def _mm_kernel(*refs, nk, n_extra, epilogue):
    a_ref, b_ref = refs[0], refs[1]
    extra = refs[2:2 + n_extra]
    o_ref = refs[2 + n_extra]
    if nk == 1:
        acc = _dot(a_ref[...], b_ref[...])
        o_ref[...] = epilogue(acc, *[e[...] for e in extra]).astype(o_ref.dtype)
    else:
        acc_ref = refs[3 + n_extra]
        k = pl.program_id(2)

        @pl.when(k == 0)
        def _():
            acc_ref[...] = jnp.zeros_like(acc_ref)

        acc_ref[...] += _dot(a_ref[...], b_ref[...])

        @pl.when(k == nk - 1)
        def _():
            o_ref[...] = epilogue(acc_ref[...], *[e[...] for e in extra]).astype(o_ref.dtype)


def _matmul(a, b, layer, *, tm, tn, tk=None, out_dtype=F32, extras=(), extra_specs=(), epilogue=None):
    m, kdim = a.shape
    n = b.shape[2]
    tm, tn = min(tm, m), min(tn, n)
    tk = kdim if tk is None else min(tk, kdim)
    nk = kdim // tk
    if epilogue is None:
        epilogue = lambda acc: acc
    kern = functools.partial(_mm_kernel, nk=nk, n_extra=len(extras), epilogue=epilogue)
    in_specs = [pl.BlockSpec((tm, tk), lambda i, j, k: (i, k)),
                pl.BlockSpec((None, tk, tn), lambda i, j, k: (layer, k, j))] + list(extra_specs)
    scratch = [] if nk == 1 else [pltpu.VMEM((tm, tn), F32)]
    return pl.pallas_call(
        kern,
        grid=(m // tm, n // tn, nk),
        in_specs=in_specs,
        out_specs=pl.BlockSpec((tm, tn), lambda i, j, k: (i, j)),
        out_shape=jax.ShapeDtypeStruct((m, n), out_dtype),
        scratch_shapes=scratch,
        compiler_params=_cparams(("parallel", "parallel", "arbitrary")),
    )(a, b, *extras)


ADA_ROWS = 8
ADA_TK = 128


def _ada_kernel(c_ref, w_ref, b_ref, o_ref):
    @pl.when(pl.program_id(0) == 0)
    def _():
        o_ref[...] = jnp.broadcast_to(b_ref[...], o_ref.shape)

    o_ref[...] += _dot(_silu(c_ref[...]).astype(BF16), w_ref[...].astype(BF16))


def _ada_projection(c, ada_w, ada_b):
    batch, d = c.shape
    n = ada_w.shape[1]
    tk = min(ADA_TK, d)
    assert batch <= ADA_ROWS
    c_pad = jnp.zeros((ADA_ROWS, d), F32).at[:batch].set(c)
    out = pl.pallas_call(
        _ada_kernel,
        grid=(d // tk,),
        in_specs=[pl.BlockSpec((ADA_ROWS, tk), lambda k: (0, k)),
                  pl.BlockSpec((tk, n), lambda k: (k, 0)),
                  pl.BlockSpec((1, n), lambda k: (0, 0))],
        out_specs=pl.BlockSpec((ADA_ROWS, n), lambda k: (0, 0)),
        out_shape=jax.ShapeDtypeStruct((ADA_ROWS, n), F32),
        compiler_params=_cparams(("arbitrary",)),
    )(c_pad, ada_w, ada_b.reshape(1, n))
    return out[:batch]


def _modulate_kernel(x_ref, sc_ref, sh_ref, u_ref):
    u_ref[...] = (x_ref[...] * (1.0 + sc_ref[0]) + sh_ref[0]).astype(u_ref.dtype)


def _modulate(x2, scale, shift, seq, ts):
    t, d = x2.shape
    ts = min(ts, seq)
    per_b = lambda i: ((i * ts) // seq, 0, 0)
    return pl.pallas_call(
        _modulate_kernel,
        grid=(t // ts,),
        in_specs=[pl.BlockSpec((ts, d), lambda i: (i, 0)),
                  pl.BlockSpec((1, 1, d), per_b),
                  pl.BlockSpec((1, 1, d), per_b)],
        out_specs=pl.BlockSpec((ts, d), lambda i: (i, 0)),
        out_shape=jax.ShapeDtypeStruct((t, d), BF16),
        compiler_params=_cparams(("parallel",)),
    )(x2, scale, shift)


def _ln_kernel(*refs, alpha, emit_u):
    if emit_u:
        x_ref, h_ref, gate_ref, g_ref, b_ref, sc_ref, sh_ref, xo_ref, uo_ref = refs
    else:
        x_ref, h_ref, gate_ref, g_ref, b_ref, xo_ref = refs
    v = alpha * x_ref[...] + (1.0 + gate_ref[0]) * h_ref[...]
    mu = jnp.mean(v, axis=-1, keepdims=True)
    vc = v - mu
    var = jnp.mean(vc * vc, axis=-1, keepdims=True)
    xn = vc * lax.rsqrt(var + LN_EPS) * g_ref[...] + b_ref[...]
    xo_ref[...] = xn
    if emit_u:
        uo_ref[...] = (xn * (1.0 + sc_ref[0]) + sh_ref[0]).astype(uo_ref.dtype)


def _deepnorm_ln(x2, h2, gate, ln_g, ln_b, seq, ts, alpha, next_scale=None, next_shift=None):
    t, d = x2.shape
    ts = min(ts, seq)
    emit_u = next_scale is not None
    per_b = lambda i: ((i * ts) // seq, 0, 0)
    row = pl.BlockSpec((ts, d), lambda i: (i, 0))
    vec_b = pl.BlockSpec((1, 1, d), per_b)
    vec = pl.BlockSpec((1, d), lambda i: (0, 0))
    in_specs = [row, row, vec_b, vec, vec]
    args = [x2, h2, gate, ln_g.reshape(1, d), ln_b.reshape(1, d)]
    out_specs = [row]
    out_shape = [jax.ShapeDtypeStruct((t, d), F32)]
    if emit_u:
        in_specs += [vec_b, vec_b]
        args += [next_scale, next_shift]
        out_specs.append(row)
        out_shape.append(jax.ShapeDtypeStruct((t, d), BF16))
    res = pl.pallas_call(
        functools.partial(_ln_kernel, alpha=alpha, emit_u=emit_u),
        grid=(t // ts,),
        in_specs=in_specs,
        out_specs=out_specs,
        out_shape=out_shape,
        compiler_params=_cparams(("parallel",)),
    )(*args)
    return (res[0], res[1]) if emit_u else (res[0], None)


GDN_PREP_TN = 512


def _gdn_prep_kernel(cur_ref, halo_ref, w_ref, o_ref, buf_ref, *, ts, seq):
    i = pl.program_id(0)
    j = pl.program_id(1)
    first = (i * ts) % seq == 0
    buf_ref[0:8, :] = jnp.where(first, 0.0, halo_ref[...])
    buf_ref[8:8 + ts, :] = cur_ref[...]
    w = w_ref[...]
    y = w[3:4] * buf_ref[8:8 + ts, :]
    for tap in range(GDN_CONV - 1):
        y = y + w[tap:tap + 1] * buf_ref[5 + tap:5 + tap + ts, :]
    y = _silu(y)
    heads_per_blk = GDN_PREP_TN // HEAD_DIM
    is_q = j < GDN_W // GDN_PREP_TN
    is_qk = j < 2 * GDN_W // GDN_PREP_TN
    for h in range(heads_per_blk):
        yh = y[:, h * HEAD_DIM:(h + 1) * HEAD_DIM]
        r = lax.rsqrt(jnp.sum(yh * yh, axis=-1, keepdims=True) + RMS_EPS)
        scale = jnp.where(is_q, r * HEAD_DIM ** -0.5, jnp.where(is_qk, r, 1.0))
        o_ref[:, h * HEAD_DIM:(h + 1) * HEAD_DIM] = (yh * scale).astype(o_ref.dtype)


def _gdn_prep(proj, conv_w, seq, ts):
    t = proj.shape[0]
    ts = min(ts, seq)
    tn = GDN_PREP_TN
    kern = functools.partial(_gdn_prep_kernel, ts=ts, seq=seq)
    return pl.pallas_call(
        kern,
        grid=(t // ts, 3 * GDN_W // tn),
        in_specs=[pl.BlockSpec((ts, tn), lambda i, j: (i, j)),
                  pl.BlockSpec((8, tn), lambda i, j: (jnp.maximum(i * (ts // 8) - 1, 0), j)),
                  pl.BlockSpec((GDN_CONV, tn), lambda i, j: (0, j))],
        out_specs=pl.BlockSpec((ts, tn), lambda i, j: (i, j)),
        out_shape=jax.ShapeDtypeStruct((t, 3 * GDN_W), BF16),
        scratch_shapes=[pltpu.VMEM((ts + 8, tn), F32)],
        compiler_params=_cparams(("parallel", "parallel")),
    )(proj, proj, conv_w)


GDN_GROUP = 256
GDN_HB = 2
GDN_NG = 2


def _bmm(a, b):
    return lax.dot_general(a, b, (((2,), (1,)), ((0,), (0,))), preferred_element_type=F32)


def _bmm_nt(a, b):
    return lax.dot_general(a, b, (((2,), (2,)), ((0,), (0,))), preferred_element_type=F32)


def _unit_lower_inverse(a, ri, ci):
    grp = a.shape[-1]
    eye = (ri == ci).astype(F32)
    blk16 = (ri // 16) == (ci // 16)
    blk32 = (ri // 32) == (ci // 32)
    d = jnp.where(blk16, a, 0.0)
    x = eye - d
    d_bf = d.astype(BF16)
    p = _bmm(d_bf, d_bf)
    for _ in range(2):
        p_bf = p.astype(BF16)
        r = _bmm(jnp.concatenate([x.astype(BF16), p_bf], axis=1), p_bf)
        x = x + r[:, :grp]
        p = r[:, grp:]
    x = x + _bmm(x.astype(BF16), p.astype(BF16))
    for e in (jnp.where(blk16, 0.0, jnp.where(blk32, a, 0.0)), jnp.where(blk32, 0.0, a)):
        x_bf = x.astype(BF16)
        x = x - _bmm(_bmm(x_bf, e.astype(BF16)).astype(BF16), x_bf)
    return x


def _gdn_chunk_kernel(q_ref, k_ref, v_ref, z_ref, small_ref, alog_ref, dtb_ref, nw_ref, o_ref, smt_ref,
                      *, n_trips):
    grp, c_len, hb, ng = GDN_GROUP, GDN_CHUNK, GDN_HB, GDN_NG
    cpg = grp // c_len
    span = ng * grp
    ri = lax.broadcasted_iota(jnp.int32, (grp, grp), 0)
    ci = lax.broadcasted_iota(jnp.int32, (grp, grp), 1)
    same = (ri // c_len) == (ci // c_len)
    eye = ri == ci
    incl = jnp.logical_and(same, ri >= ci)
    incl_t = jnp.logical_and(same, ci >= ri)
    col_chunk = lax.broadcasted_iota(jnp.int32, (HEAD_DIM, grp), 1) // c_len
    nw = nw_ref[...]
    h0 = pl.program_id(1) * hb
    a_coef = jnp.stack([-jnp.exp(alog_ref[hh][:, 0:1]) for _ in range(ng) for hh in range(hb)])
    dtb = jnp.stack([dtb_ref[hh][:, 0:1] for _ in range(ng) for hh in range(hb)])

    def body(it, state):
        r0 = pl.multiple_of(it * span, span)

        def load(ref):
            return jnp.stack([ref[pl.ds(r0 + gg * grp, grp), hh * HEAD_DIM:(hh + 1) * HEAD_DIM]
                              for gg in range(ng) for hh in range(hb)])

        for gg in range(ng):
            smt_ref[gg] = small_ref[pl.ds(r0 + gg * grp, grp), :].T

        def load_row(base):
            return jnp.stack([smt_ref[gg, pl.ds(base + h0 + hh, 1), :] for gg in range(ng) for hh in range(hb)])

        q_b, k_b, v_b = load(q_ref), load(k_ref), load(v_ref)
        kf = k_b.astype(F32)
        pa = load_row(SM_PA) + dtb
        softplus = jnp.maximum(pa, 0.0) + jnp.log(1.0 + jnp.exp(-jnp.abs(pa)))
        la_row = a_coef * softplus
        beta_row = _sigmoid(load_row(SM_PB))
        nb = la_row.shape[0]
        la_b = jnp.broadcast_to(la_row, (nb, grp, grp))
        g_col = jnp.sum(jnp.where(incl, la_b, 0.0), axis=2, keepdims=True)
        gl_col = jnp.sum(jnp.where(same, la_b, 0.0), axis=2, keepdims=True)
        la_col = jnp.sum(jnp.where(eye, la_b, 0.0), axis=2, keepdims=True)
        beta_col = jnp.sum(jnp.where(eye, jnp.broadcast_to(beta_row, (nb, grp, grp)), 0.0),
                           axis=2, keepdims=True)
        g_row = jnp.sum(jnp.where(incl_t, jnp.broadcast_to(la_col, (nb, grp, grp)), 0.0),
                        axis=1, keepdims=True)
        decay = jnp.where(incl, jnp.exp(jnp.where(incl, g_col - g_row, 0.0)), 0.0)
        kq = _bmm_nt(jnp.concatenate([k_b, q_b], axis=1), k_b)
        a_mat = jnp.where(eye, 0.0, beta_col * kq[:, :grp] * decay)
        t_inv = _unit_lower_inverse(a_mat, ri, ci).astype(BF16)
        rhs = jnp.concatenate([(beta_col * jnp.exp(g_col)) * kf, beta_col * v_b.astype(F32)], axis=2)
        wu = _bmm(t_inv, rhs.astype(BF16)).astype(BF16)
        qk = (kq[:, grp:] * decay).astype(BF16)
        qk_wu = _bmm(qk, wu)
        q_eff = (q_b.astype(F32) * jnp.exp(g_col) - qk_wu[:, :, :HEAD_DIM]).astype(BF16)
        o_loc = qk_wu[:, :, HEAD_DIM:]
        k_end = (kf * jnp.exp(gl_col - g_col)).astype(BF16)
        k_end_t = jnp.stack([k_end[b].T for b in range(nb)])
        chunk_decay = jnp.exp(gl_col)
        trans = [_bmm(jnp.where(col_chunk == cc, k_end_t, jnp.zeros_like(k_end_t)), wu) for cc in range(cpg)]

        outs = []
        for gg in range(ng):
            ent = slice(gg * hb, (gg + 1) * hb)
            for cc in range(cpg):
                rows = slice(cc * c_len, (cc + 1) * c_len)
                lhs = jnp.concatenate([q_eff[ent, rows], trans[cc][ent, :, :HEAD_DIM].astype(BF16)], axis=1)
                r = _bmm(lhs, state.astype(BF16))
                outs.append(r[:, :c_len] + o_loc[ent, rows])
                state = (chunk_decay[ent, cc * c_len:cc * c_len + 1] * state - r[:, c_len:]
                         + trans[cc][ent, :, HEAD_DIM:])
        o = jnp.concatenate(outs, axis=1)
        o = o * lax.rsqrt(jnp.mean(o * o, axis=-1, keepdims=True) + RMS_EPS) * nw
        o_all = jnp.concatenate([o[hh] for hh in range(hb)], axis=1) * _silu(z_ref[pl.ds(r0, span), :])
        o_ref[pl.ds(r0, span), :] = o_all.astype(o_ref.dtype)
        return state

    lax.fori_loop(0, n_trips, body, jnp.zeros((hb, HEAD_DIM, HEAD_DIM), F32))


def _gdn_chunk(qkv, proj, small, a_log, dt_bias, norm_w, batch, seq):
    t = qkv.shape[0]
    n_groups = seq // GDN_GROUP
    assert n_groups % GDN_NG == 0
    width = GDN_HB * HEAD_DIM
    nhb = GDN_HEADS // GDN_HB
    blk = lambda off: pl.BlockSpec((seq, width), lambda b, h: (b, off + h))
    head_vec = pl.BlockSpec((GDN_HB, 1, HEAD_DIM), lambda b, h: (h, 0, 0))
    kern = functools.partial(_gdn_chunk_kernel, n_trips=n_groups // GDN_NG)
    return pl.pallas_call(
        kern,
        grid=(batch, nhb),
        in_specs=[blk(0), blk(nhb), blk(2 * nhb),
                  pl.BlockSpec((seq, width), lambda b, h: (b, C_Z // width + h)),
                  pl.BlockSpec((seq, SMALL_W), lambda b, h: (b, 0)),
                  head_vec, head_vec,
                  pl.BlockSpec((1, HEAD_DIM), lambda b, h: (0, 0))],
        out_specs=pl.BlockSpec((seq, width), lambda b, h: (b, h)),
        out_shape=jax.ShapeDtypeStruct((t, GDN_W), BF16),
        scratch_shapes=[pltpu.VMEM((GDN_NG, SMALL_W, GDN_GROUP), F32)],
        compiler_params=_cparams(("parallel", "parallel")),
    )(qkv, qkv, qkv, proj, small, a_log, dt_bias, norm_w)


def _cmp_kernel(x_ref, w1_ref, pos_ref, w2_ref, o_ref):
    ns = x_ref.shape[0] // CMP_STRIDE
    both = jnp.zeros((ns, 2 * HEAD_DIM), F32)
    pos_term = jnp.zeros((8, HEAD_DIM), F32)
    for r in range(CMP_STRIDE):
        w_r = w1_ref[0, r]
        both = both + _dot(x_ref[pl.ds(r, ns, stride=CMP_STRIDE), :].astype(BF16), w_r)
        for half in range(2):
            p_row = jnp.broadcast_to(pos_ref[0, half * CMP_STRIDE + r:half * CMP_STRIDE + r + 1, :],
                                     (8, HEAD_DIM)).astype(BF16)
            pos_term = pos_term + _dot(p_row, w_r[:, half * HEAD_DIM:(half + 1) * HEAD_DIM])
    first, second = both[:, :HEAD_DIM], both[:, HEAD_DIM:]
    hid = _silu(first + pltpu.roll(second, ns - 1, axis=0) + pos_term[0:1])
    out = _dot(hid.astype(BF16), w2_ref[0])
    rows = lax.broadcasted_iota(jnp.int32, out.shape, 0)
    o_ref[0, 0, 0] = jnp.where(rows < ns - 1, out, 0.0)


def _nsa_compress(proj, w1cat, pos, w2, batch, seq):
    ns = seq // CMP_STRIDE
    g = NSA_GROUPS
    return pl.pallas_call(
        _cmp_kernel,
        grid=(batch, 2, g),
        in_specs=[pl.BlockSpec((seq, HEAD_DIM),
                               lambda bi, kv, gi: (bi, C_KV // HEAD_DIM + kv * NSA_GROUPS + gi)),
                  pl.BlockSpec((1, CMP_STRIDE, HEAD_DIM, 2 * HEAD_DIM), lambda bi, kv, gi: (kv, 0, 0, 0)),
                  pl.BlockSpec((1, CMP_BLOCK, HEAD_DIM), lambda bi, kv, gi: (kv, 0, 0)),
                  pl.BlockSpec((1, HEAD_DIM, HEAD_DIM), lambda bi, kv, gi: (kv, 0, 0))],
        out_specs=pl.BlockSpec((1, 1, 1, ns, HEAD_DIM), lambda bi, kv, gi: (bi, kv, gi, 0, 0)),
        out_shape=jax.ShapeDtypeStruct((batch, 2, g, ns, HEAD_DIM), F32),
        compiler_params=_cparams(("parallel", "parallel", "parallel")),
    )(proj, w1cat, pos, w2)


NSA_TQ = 256
NSA_KC = 512
NSA_NBP = 128
NSA_MAX_BLOCKS = 64
FEAT_POS_HI = 64
FEAT_POS_LO = 67
SLOPE_TERMS = 3
MASK_BIG = 1e30
LOG2E = 1.4426950408889634


def _masked_softmax(s, valid):
    s = jnp.where(valid, s, NEG_INF)
    m = jnp.max(s, axis=-1, keepdims=True)
    p = jnp.where(valid, jnp.exp(s - m), 0.0)
    l = jnp.sum(p, axis=-1, keepdims=True)
    return p / jnp.where(l > 0.0, l, 1.0)


def _key_features(seq):
    pos = np.arange(seq)
    f = np.zeros((seq, HEAD_DIM), np.float32)
    f[pos, pos // SLC_BLOCK] = 1.0
    f[:, FEAT_POS_HI:FEAT_POS_HI + SLOPE_TERMS] = (SLC_BLOCK * (pos // SLC_BLOCK))[:, None]
    f[:, FEAT_POS_LO:FEAT_POS_LO + SLOPE_TERMS] = (pos % SLC_BLOCK)[:, None]
    return jnp.asarray(f, dtype=BF16)


def _nsa_kernel(q_ref, kc_ref, vc_ref, ks_ref, vs_ref, kw_ref, vw_ref, sm_ref, c2s_ref, feat_ref, o_ref,
                *, tq, seq, kchunk):
    g = pl.program_id(1)
    i = pl.program_id(2)
    t0 = i * tq
    rows = HPG * tq
    ns = kc_ref.shape[-2]
    nbp = NSA_NBP
    top_n = min(SLC_TOPK, seq // SLC_BLOCK)

    q = q_ref[...]
    q32 = jnp.concatenate([q[:, h * HEAD_DIM:(h + 1) * HEAD_DIM] for h in range(HPG)], axis=0)
    q32 = q32 * HEAD_DIM ** -0.5
    qr = q32.astype(BF16)
    qr2 = (q32 * LOG2E).astype(BF16)
    row = lax.broadcasted_iota(jnp.int32, (rows, 1), 0)
    hh = row // tq
    tf = (t0 + row - hh * tq).astype(F32)
    slope = jnp.exp2(-0.5 * (HPG * g + hh + 1).astype(F32))

    kcm = kc_ref[0, 0, 0].astype(BF16)
    vcm = vc_ref[0, 0, 0].astype(BF16)
    jj = lax.broadcasted_iota(jnp.int32, (1, ns), 1)
    dist = tf - (jj * CMP_STRIDE + (CMP_BLOCK - 1)).astype(F32)
    p_cmp = _masked_softmax(_dot_nt(qr, kcm) - slope * dist, dist >= 0.0)
    o_cmp = _dot(p_cmp.astype(BF16), vcm)
    p_sum = p_cmp[0:tq]
    for h in range(1, HPG):
        p_sum = p_sum + p_cmp[h * tq:(h + 1) * tq]
    p_hi, p_lo = _split(p_sum)
    c2s_t = c2s_ref[...]
    imp_t = _dot_nt(c2s_t, p_hi) + _dot_nt(c2s_t, p_lo)

    nbr = NSA_MAX_BLOCKS
    tt = t0 + lax.broadcasted_iota(jnp.int32, (1, tq), 1)
    cur = tt // SLC_BLOCK
    blk = lax.broadcasted_iota(jnp.int32, (nbr, 1), 0)
    blk_f = blk.astype(F32)
    allowed = blk * SLC_BLOCK <= tt
    forced = jnp.logical_or(blk == 0, jnp.logical_or(blk == cur, blk == cur - 1))
    sc = jnp.where(forced, FORCED_SCORE, jnp.where(allowed, imp_t[:nbr], NEG_INF))
    sel_t = jnp.zeros((nbr, tq), F32)
    for _ in range(top_n):
        mx = jnp.max(sc, axis=0, keepdims=True)
        idx = jnp.min(jnp.where(sc == mx, blk_f, float(nbr)), axis=0, keepdims=True)
        pick = blk_f == idx
        sel_t = jnp.where(pick, 1.0, sel_t)
        sc = jnp.where(pick, -jnp.inf, sc)
    sel_t = jnp.where(allowed, sel_t, 0.0)
    sel = jnp.concatenate([sel_t, jnp.zeros((nbp - nbr, tq), F32)], axis=0).T
    sel_bias = jnp.concatenate([(sel - 1.0) * MASK_BIG] * HPG, axis=0)

    lane = lax.broadcasted_iota(jnp.int32, (1, HEAD_DIM), 1)
    sl2 = slope * LOG2E
    terms = []
    rem = sl2
    for _ in range(SLOPE_TERMS):
        term = rem.astype(BF16).astype(F32)
        terms.append(term)
        rem = rem - term
    slope_feat = jnp.zeros((rows, HEAD_DIM), F32)
    for n, term in enumerate(terms):
        hit = jnp.logical_or(lane == FEAT_POS_HI + n, lane == FEAT_POS_LO + n)
        slope_feat = jnp.where(hit, term, slope_feat)
    q_slc = jnp.concatenate([qr2, jnp.where(lane < NSA_MAX_BLOCKS, sel_bias, slope_feat).astype(BF16)], axis=1)
    q_win = jnp.concatenate([qr2, slope_feat.astype(BF16)], axis=1)

    col = lax.broadcasted_iota(jnp.int32, (1, rows), 1)
    t_col = (t0 + col - (col // tq) * tq).astype(F32)
    key_sub = lax.broadcasted_iota(jnp.int32, (kchunk, 1), 0)

    def slc_chunk(c, carry, causal):
        m, l, acc = carry
        k0 = pl.multiple_of(c * kchunk, kchunk)
        kb = jnp.concatenate([ks_ref[pl.ds(k0, kchunk), :].astype(BF16), feat_ref[pl.ds(k0, kchunk), :]], axis=1)
        s = _dot_nt(kb, q_slc)
        if causal:
            s = jnp.where((k0 + key_sub).astype(F32) <= t_col, s, -MASK_BIG)
        m_new = jnp.maximum(m, jnp.max(s, axis=0, keepdims=True))
        alpha = jnp.exp2(m - m_new)
        p = jnp.exp2(s - m_new)
        l = alpha * l + jnp.sum(p, axis=0, keepdims=True)
        v_t = vs_ref[pl.ds(k0, kchunk), :].T.astype(BF16)
        acc = alpha * acc + _dot(v_t, p.astype(BF16))
        return m_new, l, acc

    n_full = t0 // kchunk
    carry = lax.fori_loop(
        0, n_full, lambda c, cr: slc_chunk(c, cr, False),
        (jnp.full((1, rows), NEG_INF, F32), jnp.zeros((1, rows), F32), jnp.zeros((HEAD_DIM, rows), F32)))
    _, l_s, acc_s = slc_chunk(n_full, carry, True)
    o_slc = (acc_s / l_s).T

    wk = WINDOW + tq

    def window_attend(w0, mask_fn):
        kb = jnp.concatenate([kw_ref[pl.ds(w0, wk), :].astype(BF16), feat_ref[pl.ds(w0, wk), :]], axis=1)
        pos = (w0 + lax.broadcasted_iota(jnp.int32, (wk, 1), 0)).astype(F32)
        s = mask_fn(_dot_nt(kb, q_win), pos)
        p = jnp.exp2(s - jnp.max(s, axis=0, keepdims=True))
        v_t = vw_ref[pl.ds(w0, wk), :].T.astype(BF16)
        o_t = _dot(v_t, p.astype(BF16)) / jnp.sum(p, axis=0, keepdims=True)
        return o_t.T

    def banded(s, pos):
        left = jnp.where(pos[:tq] > t_col - float(WINDOW), s[:tq], -MASK_BIG)
        right = jnp.where(pos[wk - tq:] <= t_col, s[wk - tq:], -MASK_BIG)
        return jnp.concatenate([left, s[tq:wk - tq], right], axis=0)

    def head_of_sequence(s, pos):
        return jnp.where(pos <= t_col, s, -MASK_BIG)

    o_win = lax.cond(t0 >= WINDOW,
                     lambda: window_attend(pl.multiple_of(t0 - WINDOW, tq), banded),
                     lambda: window_attend(0, head_of_sequence))

    gt = _sigmoid(sm_ref[...])
    lane_s = lax.broadcasted_iota(jnp.int32, (1, SMALL_W), 1)
    branches = (o_cmp, o_slc, o_win)
    for h in range(HPG):
        out_h = jnp.zeros((tq, HEAD_DIM), F32)
        for br in range(3):
            col = SM_NG + br * NSA_HEADS + g * HPG + h
            gate = jnp.sum(jnp.where(lane_s == col, gt, 0.0), axis=-1, keepdims=True)
            out_h = out_h + gate * branches[br][h * tq:(h + 1) * tq]
        o_ref[:, h * HEAD_DIM:(h + 1) * HEAD_DIM] = out_h.astype(o_ref.dtype)


def _nsa_attend(proj, small, cmp_kv, c2s, feats, batch, seq):
    t = proj.shape[0]
    tq = min(NSA_TQ, seq)
    kchunk = min(NSA_KC, seq)
    nq = seq // tq
    ns = cmp_kv.shape[-2]
    assert seq // SLC_BLOCK <= NSA_MAX_BLOCKS and WINDOW % tq == 0 and kchunk % tq == 0 and WINDOW + tq <= seq
    kv_blk = lambda idx: pl.BlockSpec(
        (seq, HEAD_DIM), lambda b, g, i: (b, (C_KV + idx * NSA_KW) // HEAD_DIM + g))
    cmp_blk = lambda kv: pl.BlockSpec((1, 1, 1, ns, HEAD_DIM), lambda b, g, i: (b, kv, g, 0, 0))
    kern = functools.partial(_nsa_kernel, tq=tq, seq=seq, kchunk=kchunk)
    return pl.pallas_call(
        kern,
        grid=(batch, NSA_GROUPS, nq),
        in_specs=[pl.BlockSpec((tq, NSA_KW), lambda b, g, i: (b * nq + i, C_NQ // NSA_KW + g)),
                  cmp_blk(0), cmp_blk(1),
                  kv_blk(2), kv_blk(3), kv_blk(4), kv_blk(5),
                  pl.BlockSpec((tq, SMALL_W), lambda b, g, i: (b * nq + i, 0)),
                  pl.BlockSpec((NSA_NBP, ns), lambda b, g, i: (0, 0)),
                  pl.BlockSpec((seq, HEAD_DIM), lambda b, g, i: (0, 0))],
        out_specs=pl.BlockSpec((tq, NSA_KW), lambda b, g, i: (b * nq + i, g)),
        out_shape=jax.ShapeDtypeStruct((t, NSA_QW), BF16),
        compiler_params=_cparams(("parallel", "parallel", "arbitrary")),
    )(proj, cmp_kv, cmp_kv, proj, proj, proj, proj, small, c2s, feats)


def _readout_kernel(oa_ref, ob_ref, wa_ref, wb_ref, ma_ref, mb_ref, y_ref):
    ya = _dot(oa_ref[...], wa_ref[...])
    yb = _dot(ob_ref[...], wb_ref[...])
    y_ref[...] = (_sigmoid(ma_ref[...]) * ya + _sigmoid(mb_ref[...]) * yb).astype(y_ref.dtype)


def _readout(o_a, o_b, w_a, w_b, layer, proj, d_model, tm, tn):
    t = o_a.shape[0]
    tm, tn = min(tm, t), min(tn, d_model)
    c_mb = C_MA + d_model
    return pl.pallas_call(
        _readout_kernel,
        grid=(t // tm, d_model // tn),
        in_specs=[pl.BlockSpec((tm, GDN_W), lambda i, j: (i, 0)),
                  pl.BlockSpec((tm, NSA_QW), lambda i, j: (i, 0)),
                  pl.BlockSpec((None, GDN_W, tn), lambda i, j: (layer, 0, j)),
                  pl.BlockSpec((None, NSA_QW, tn), lambda i, j: (layer, 0, j)),
                  pl.BlockSpec((tm, tn), lambda i, j: (i, C_MA // tn + j)),
                  pl.BlockSpec((tm, tn), lambda i, j: (i, c_mb // tn + j))],
        out_specs=pl.BlockSpec((tm, tn), lambda i, j: (i, j)),
        out_shape=jax.ShapeDtypeStruct((t, d_model), BF16),
        compiler_params=_cparams(("parallel", "parallel")),
    )(o_a, o_b, w_a, w_b, proj, proj)


def _cmp_to_slc(seq):
    ns = seq // CMP_STRIDE
    nb = seq // SLC_BLOCK
    start = np.arange(ns) * CMP_STRIDE
    end = start + CMP_BLOCK - 1
    s0 = np.arange(nb) * SLC_BLOCK
    m = (end[:, None] >= s0[None, :]) & (start[:, None] <= s0[None, :] + SLC_BLOCK - 1)
    m[ns - 1] = False
    out = np.zeros((NSA_NBP, ns), np.float32)
    out[:nb] = m.T
    return jnp.asarray(out, dtype=BF16)


PERM_TR = 1024
PERM_TAIL = 128


def _permute_kernel(a_ref, b_ref, o_ref, *, ranges):
    j = pl.program_id(1)
    x = jnp.concatenate([a_ref[...], b_ref[...]], axis=1)
    for lo, hi, shift in ranges:
        @pl.when(jnp.logical_and(j >= lo, j < hi))
        def _():
            o_ref[...] = x[:, shift:shift + PROJ_TN].astype(o_ref.dtype)


def _permute_w_in(w_in, d_model):
    depth, d, _ = w_in.shape
    g0 = 4 * GDN_W
    s0 = g0 + 2 * GDN_HEADS
    s1 = s0 + NSA_QW + 6 * NSA_KW
    s2 = s1 + 3 * NSA_HEADS
    n_main = C_MA + 2 * d_model
    bounds = (0, g0 // PROJ_TN, C_MA // PROJ_TN, n_main // PROJ_TN)
    shifts = (0, s0 - g0, s2 - C_MA)
    assert g0 % PROJ_TN == 0 and C_MA % PROJ_TN == 0 and n_main % PROJ_TN == 0 and max(shifts) <= PERM_TAIL
    ranges = tuple((bounds[r], bounds[r + 1], shifts[r]) for r in range(3))
    tr = min(PERM_TR, d)
    tail_per_blk = PROJ_TN // PERM_TAIL
    main = pl.pallas_call(
        functools.partial(_permute_kernel, ranges=ranges),
        grid=(depth, n_main // PROJ_TN, d // tr),
        in_specs=[pl.BlockSpec((None, tr, PROJ_TN), lambda l, j, r: (l, r, j)),
                  pl.BlockSpec((None, tr, PERM_TAIL), lambda l, j, r: (l, r, (j + 1) * tail_per_blk))],
        out_specs=pl.BlockSpec((None, tr, PROJ_TN), lambda l, j, r: (l, r, j)),
        out_shape=jax.ShapeDtypeStruct((depth, d, n_main), BF16),
        compiler_params=_cparams(("parallel", "parallel", "parallel")),
    )(w_in, w_in)
    small = jnp.concatenate([w_in[..., g0:s0], w_in[..., s1:s2],
                             jnp.zeros((depth, d, SMALL_W - (s0 - g0) - (s2 - s1)), w_in.dtype)], axis=-1)
    return main, small.astype(BF16)


def _mixer(u, layer, proj_w, small_w, conv_w, a_log, dt_bias, norm_w, cmp_pos, cmp_w1, cmp_w2, w_read_a, w_read_b,
           w_out, c2s, feats, batch, seq, d_model):
    proj = _matmul(u, proj_w, layer, tm=1024, tn=PROJ_TN)
    small = _matmul(u, small_w, layer, tm=1024, tn=SMALL_W)

    qkv = _gdn_prep(proj, conv_w, seq, ts=512)
    bcast = lambda v: jnp.broadcast_to(v.reshape(GDN_HEADS, 1, 1), (GDN_HEADS, 1, HEAD_DIM))
    o_a = _gdn_chunk(qkv, proj, small, bcast(a_log), bcast(dt_bias), norm_w.reshape(1, HEAD_DIM), batch, seq)

    w1cat = jnp.concatenate([cmp_w1[:, :CMP_STRIDE], cmp_w1[:, CMP_STRIDE:]], axis=-1).astype(BF16)
    cmp_kv = _nsa_compress(proj, w1cat, cmp_pos, cmp_w2.astype(BF16), batch, seq)
    o_b = _nsa_attend(proj, small, cmp_kv, c2s, feats, batch, seq)

    y = _readout(o_a, o_b, w_read_a, w_read_b, layer, proj, d_model, tm=512, tn=512)
    return _matmul(y, w_out, layer, tm=1024, tn=512)


def kernel(x, c, ada_w, ada_b, ada_table, w_in, gdn_conv_w, gdn_a_log, gdn_dt_bias, gdn_norm_w,
           cmp_pos, cmp_w1, cmp_w2, w_read_a, w_read_b, w_out, ln1_g, ln1_b,
           mlp_w1, mlp_b1, mlp_w2, mlp_b2, ln2_g, ln2_b):
    batch, seq, d_model = x.shape
    depth = w_in.shape[0]
    d_ff = mlp_w1.shape[-1]
    t = batch * seq
    alpha = (2.0 * depth) ** 0.25
    c2s = _cmp_to_slc(seq)
    feats = _key_features(seq)

    proj_w, small_w = _permute_w_in(w_in, d_model)
    w_read_a, w_read_b, w_out = (w.astype(BF16) for w in (w_read_a, w_read_b, w_out))
    mlp_w1, mlp_w2 = mlp_w1.astype(BF16), mlp_w2.astype(BF16)

    mod = _ada_projection(c, ada_w, ada_b)

    x2 = x.reshape(t, d_model)
    u = None
    for l in range(depth):
        m = (mod + ada_table[l].reshape(1, -1)).reshape(batch, N_ADA, 1, d_model)
        shift1, scale1, gate1, shift2, scale2, gate2 = (m[:, j] for j in range(N_ADA))
        if u is None:
            u = _modulate(x2, scale1, shift1, seq, ts=512)
        h = _mixer(u, l, proj_w, small_w, gdn_conv_w[l], gdn_a_log[l], gdn_dt_bias[l],
                   gdn_norm_w[l], cmp_pos[l], cmp_w1[l], cmp_w2[l], w_read_a, w_read_b, w_out,
                   c2s, feats, batch, seq, d_model)
        x2, u = _deepnorm_ln(x2, h, gate1, ln1_g[l], ln1_b[l], seq, 256, alpha, scale2, shift2)

        tn1 = min(1024, d_ff)
        act = _matmul(u, mlp_w1, l, tm=1024, tn=tn1, out_dtype=BF16,
                      extras=(mlp_b1[l].reshape(1, d_ff),),
                      extra_specs=(pl.BlockSpec((1, tn1), lambda i, j, k: (0, j)),),
                      epilogue=lambda acc, b: jnp.square(jnp.maximum(acc + b, 0.0)))
        tn2 = min(1024, d_model)
        h = _matmul(act, mlp_w2, l, tm=1024, tn=tn2, tk=4096,
                    extras=(mlp_b2[l].reshape(1, d_model),),
                    extra_specs=(pl.BlockSpec((1, tn2), lambda i, j, k: (0, j)),),
                    epilogue=lambda acc, b: acc + b)
        if l + 1 < depth:
            m_next = (mod + ada_table[l + 1].reshape(1, -1)).reshape(batch, N_ADA, 1, d_model)
            x2, u = _deepnorm_ln(x2, h, gate2, ln2_g[l], ln2_b[l], seq, 256, alpha,
                                 m_next[:, 1], m_next[:, 0])
        else:
            x2, _ = _deepnorm_ln(x2, h, gate2, ln2_g[l], ln2_b[l], seq, 256, alpha)
    return x2.reshape(batch, seq, d_model)
```

```python
import functools

import numpy as np
import jax
import jax.numpy as jnp
from jax import lax
from jax.experimental import pallas as pl
from jax.experimental.pallas import tpu as pltpu

F32 = jnp.float32
BF16 = jnp.bfloat16

HEAD_DIM = 128
GDN_HEADS = 16
GDN_CONV = 4
GDN_CHUNK = 64
NSA_HEADS = 16
NSA_GROUPS = 4
HPG = NSA_HEADS // NSA_GROUPS
CMP_BLOCK = 32
CMP_STRIDE = 16
SLC_BLOCK = 64
SLC_TOPK = 16
WINDOW = 512
N_ADA = 6
LN_EPS = 1e-5
RMS_EPS = 1e-6
NEG_INF = -1e30
FORCED_SCORE = 1e6

GDN_W = GDN_HEADS * HEAD_DIM
NSA_QW = NSA_HEADS * HEAD_DIM
NSA_KW = NSA_GROUPS * HEAD_DIM

C_QKV = 0
C_Z = 3 * GDN_W
C_NQ = C_Z + GDN_W
C_KV = C_NQ + NSA_QW
C_MA = C_KV + 6 * NSA_KW
SMALL_W = 128
SM_PB, SM_PA, SM_NG = 0, GDN_HEADS, 2 * GDN_HEADS
PROJ_TN = 512

VMEM_LIMIT = 56 * 1024 * 1024


def _cparams(sem):
    return pltpu.CompilerParams(dimension_semantics=sem, vmem_limit_bytes=VMEM_LIMIT)


def _sigmoid(x):
    return 1.0 / (1.0 + jnp.exp(-x))


def _silu(x):
    return x * _sigmoid(x)


def _dot(a, b):
    return jnp.dot(a, b, preferred_element_type=F32)


def _dot_nt(a, b):
    return lax.dot_general(a, b, (((1,), (1,)), ((), ())), preferred_element_type=F32)


def _split(a):
    hi = a.astype(BF16)
    lo = (a - hi.astype(F32)).astype(BF16)
    return hi, lo


def _mm_kernel(*refs, nk, n_extra, epilogue):
    a_ref, b_ref = refs[0], refs[1]
    extra = refs[2:2 + n_extra]
    o_ref = refs[2 + n_extra]
    if nk == 1:
        acc = _dot(a_ref[...], b_ref[...])
        o_ref[...] = epilogue(acc, *[e[...] for e in extra]).astype(o_ref.dtype)
    else:
        acc_ref = refs[3 + n_extra]
        k = pl.program_id(2)

        @pl.when(k == 0)
        def _():
            acc_ref[...] = jnp.zeros_like(acc_ref)

        acc_ref[...] += _dot(a_ref[...], b_ref[...])

        @pl.when(k == nk - 1)
        def _():
            o_ref[...] = epilogue(acc_ref[...], *[e[...] for e in extra]).astype(o_ref.dtype)


def _matmul(a, b, layer, *, tm, tn, tk=None, out_dtype=F32, extras=(), extra_specs=(), epilogue=None):
    m, kdim = a.shape
    n = b.shape[2]
    tm, tn = min(tm, m), min(tn, n)
    tk = kdim if tk is None else min(tk, kdim)
    nk = kdim // tk
    if epilogue is None:
        epilogue = lambda acc: acc
    kern = functools.partial(_mm_kernel, nk=nk, n_extra=len(extras), epilogue=epilogue)
    in_specs = [pl.BlockSpec((tm, tk), lambda i, j, k: (i, k)),
                pl.BlockSpec((None, tk, tn), lambda i, j, k: (layer, k, j))] + list(extra_specs)
    scratch = [] if nk == 1 else [pltpu.VMEM((tm, tn), F32)]
    return pl.pallas_call(
        kern,
        grid=(m // tm, n // tn, nk),
        in_specs=in_specs,
        out_specs=pl.BlockSpec((tm, tn), lambda i, j, k: (i, j)),
        out_shape=jax.ShapeDtypeStruct((m, n), out_dtype),
        scratch_shapes=scratch,
        compiler_params=_cparams(("parallel", "parallel", "arbitrary")),
    )(a, b, *extras)


ADA_ROWS = 8
ADA_TK = 128


def _ada_kernel(c_ref, w_ref, b_ref, o_ref):
    @pl.when(pl.program_id(0) == 0)
    def _():
        o_ref[...] = jnp.broadcast_to(b_ref[...], o_ref.shape)

    o_ref[...] += _dot(_silu(c_ref[...]).astype(BF16), w_ref[...].astype(BF16))


def _ada_projection(c, ada_w, ada_b):
    batch, d = c.shape
    n = ada_w.shape[1]
    tk = min(ADA_TK, d)
    assert batch <= ADA_ROWS
    c_pad = jnp.zeros((ADA_ROWS, d), F32).at[:batch].set(c)
    out = pl.pallas_call(
        _ada_kernel,
        grid=(d // tk,),
        in_specs=[pl.BlockSpec((ADA_ROWS, tk), lambda k: (0, k)),
                  pl.BlockSpec((tk, n), lambda k: (k, 0)),
                  pl.BlockSpec((1, n), lambda k: (0, 0))],
        out_specs=pl.BlockSpec((ADA_ROWS, n), lambda k: (0, 0)),
        out_shape=jax.ShapeDtypeStruct((ADA_ROWS, n), F32),
        compiler_params=_cparams(("arbitrary",)),
    )(c_pad, ada_w, ada_b.reshape(1, n))
    return out[:batch]


def _modulate_kernel(x_ref, sc_ref, sh_ref, u_ref):
    u_ref[...] = (x_ref[...] * (1.0 + sc_ref[0]) + sh_ref[0]).astype(u_ref.dtype)


def _modulate(x2, scale, shift, seq, ts):
    t, d = x2.shape
    ts = min(ts, seq)
    per_b = lambda i: ((i * ts) // seq, 0, 0)
    return pl.pallas_call(
        _modulate_kernel,
        grid=(t // ts,),
        in_specs=[pl.BlockSpec((ts, d), lambda i: (i, 0)),
                  pl.BlockSpec((1, 1, d), per_b),
                  pl.BlockSpec((1, 1, d), per_b)],
        out_specs=pl.BlockSpec((ts, d), lambda i: (i, 0)),
        out_shape=jax.ShapeDtypeStruct((t, d), BF16),
        compiler_params=_cparams(("parallel",)),
    )(x2, scale, shift)


def _ln_kernel(*refs, alpha, emit_u):
    if emit_u:
        x_ref, h_ref, gate_ref, g_ref, b_ref, sc_ref, sh_ref, xo_ref, uo_ref = refs
    else:
        x_ref, h_ref, gate_ref, g_ref, b_ref, xo_ref = refs
    v = alpha * x_ref[...] + (1.0 + gate_ref[0]) * h_ref[...]
    mu = jnp.mean(v, axis=-1, keepdims=True)
    vc = v - mu
    var = jnp.mean(vc * vc, axis=-1, keepdims=True)
    xn = vc * lax.rsqrt(var + LN_EPS) * g_ref[...] + b_ref[...]
    xo_ref[...] = xn
    if emit_u:
        uo_ref[...] = (xn * (1.0 + sc_ref[0]) + sh_ref[0]).astype(uo_ref.dtype)


def _deepnorm_ln(x2, h2, gate, ln_g, ln_b, seq, ts, alpha, next_scale=None, next_shift=None):
    t, d = x2.shape
    ts = min(ts, seq)
    emit_u = next_scale is not None
    per_b = lambda i: ((i * ts) // seq, 0, 0)
    row = pl.BlockSpec((ts, d), lambda i: (i, 0))
    vec_b = pl.BlockSpec((1, 1, d), per_b)
    vec = pl.BlockSpec((1, d), lambda i: (0, 0))
    in_specs = [row, row, vec_b, vec, vec]
    args = [x2, h2, gate, ln_g.reshape(1, d), ln_b.reshape(1, d)]
    out_specs = [row]
    out_shape = [jax.ShapeDtypeStruct((t, d), F32)]
    if emit_u:
        in_specs += [vec_b, vec_b]
        args += [next_scale, next_shift]
        out_specs.append(row)
        out_shape.append(jax.ShapeDtypeStruct((t, d), BF16))
    res = pl.pallas_call(
        functools.partial(_ln_kernel, alpha=alpha, emit_u=emit_u),
        grid=(t // ts,),
        in_specs=in_specs,
        out_specs=out_specs,
        out_shape=out_shape,
        compiler_params=_cparams(("parallel",)),
    )(*args)
    return (res[0], res[1]) if emit_u else (res[0], None)


GDN_PREP_TN = 512


def _gdn_prep_kernel(cur_ref, halo_ref, w_ref, o_ref, buf_ref, *, ts, seq):
    i = pl.program_id(0)
    j = pl.program_id(1)
    first = (i * ts) % seq == 0
    buf_ref[0:8, :] = jnp.where(first, 0.0, halo_ref[...])
    buf_ref[8:8 + ts, :] = cur_ref[...]
    w = w_ref[...]
    y = w[3:4] * buf_ref[8:8 + ts, :]
    for tap in range(GDN_CONV - 1):
        y = y + w[tap:tap + 1] * buf_ref[5 + tap:5 + tap + ts, :]
    y = _silu(y)
    heads_per_blk = GDN_PREP_TN // HEAD_DIM
    is_q = j < GDN_W // GDN_PREP_TN
    is_qk = j < 2 * GDN_W // GDN_PREP_TN
    for h in range(heads_per_blk):
        yh = y[:, h * HEAD_DIM:(h + 1) * HEAD_DIM]
        r = lax.rsqrt(jnp.sum(yh * yh, axis=-1, keepdims=True) + RMS_EPS)
        scale = jnp.where(is_q, r * HEAD_DIM ** -0.5, jnp.where(is_qk, r, 1.0))
        o_ref[:, h * HEAD_DIM:(h + 1) * HEAD_DIM] = (yh * scale).astype(o_ref.dtype)


def _gdn_prep(proj, conv_w, seq, ts):
    t = proj.shape[0]
    ts = min(ts, seq)
    tn = GDN_PREP_TN
    kern = functools.partial(_gdn_prep_kernel, ts=ts, seq=seq)
    return pl.pallas_call(
        kern,
        grid=(t // ts, 3 * GDN_W // tn),
        in_specs=[pl.BlockSpec((ts, tn), lambda i, j: (i, j)),
                  pl.BlockSpec((8, tn), lambda i, j: (jnp.maximum(i * (ts // 8) - 1, 0), j)),
                  pl.BlockSpec((GDN_CONV, tn), lambda i, j: (0, j))],
        out_specs=pl.BlockSpec((ts, tn), lambda i, j: (i, j)),
        out_shape=jax.ShapeDtypeStruct((t, 3 * GDN_W), BF16),
        scratch_shapes=[pltpu.VMEM((ts + 8, tn), F32)],
        compiler_params=_cparams(("parallel", "parallel")),
    )(proj, proj, conv_w)


GDN_GROUP = 256
GDN_HB = 2
GDN_NG = 2


def _bmm(a, b):
    return lax.dot_general(a, b, (((2,), (1,)), ((0,), (0,))), preferred_element_type=F32)


def _bmm_nt(a, b):
    return lax.dot_general(a, b, (((2,), (2,)), ((0,), (0,))), preferred_element_type=F32)


def _unit_lower_inverse(a, ri, ci):
    grp = a.shape[-1]
    eye = (ri == ci).astype(F32)
    blk16 = (ri // 16) == (ci // 16)
    blk32 = (ri // 32) == (ci // 32)
    d = jnp.where(blk16, a, 0.0)
    x = eye - d
    d_bf = d.astype(BF16)
    p = _bmm(d_bf, d_bf)
    for _ in range(2):
        p_bf = p.astype(BF16)
        r = _bmm(jnp.concatenate([x.astype(BF16), p_bf], axis=1), p_bf)
        x = x + r[:, :grp]
        p = r[:, grp:]
    x = x + _bmm(x.astype(BF16), p.astype(BF16))
    for e in (jnp.where(blk16, 0.0, jnp.where(blk32, a, 0.0)), jnp.where(blk32, 0.0, a)):
        x_bf = x.astype(BF16)
        x = x - _bmm(_bmm(x_bf, e.astype(BF16)).astype(BF16), x_bf)
    return x


def _gdn_chunk_kernel(q_ref, k_ref, v_ref, z_ref, small_ref, alog_ref, dtb_ref, nw_ref, o_ref, smt_ref,
                      *, n_trips):
    grp, c_len, hb, ng = GDN_GROUP, GDN_CHUNK, GDN_HB, GDN_NG
    cpg = grp // c_len
    span = ng * grp
    ri = lax.broadcasted_iota(jnp.int32, (grp, grp), 0)
    ci = lax.broadcasted_iota(jnp.int32, (grp, grp), 1)
    same = (ri // c_len) == (ci // c_len)
    eye = ri == ci
    incl = jnp.logical_and(same, ri >= ci)
    incl_t = jnp.logical_and(same, ci >= ri)
    col_chunk = lax.broadcasted_iota(jnp.int32, (HEAD_DIM, grp), 1) // c_len
    nw = nw_ref[...]
    h0 = pl.program_id(1) * hb
    a_coef = jnp.stack([-jnp.exp(alog_ref[hh][:, 0:1]) for _ in range(ng) for hh in range(hb)])
    dtb = jnp.stack([dtb_ref[hh][:, 0:1] for _ in range(ng) for hh in range(hb)])

    def body(it, state):
        r0 = pl.multiple_of(it * span, span)

        def load(ref):
            return jnp.stack([ref[pl.ds(r0 + gg * grp, grp), hh * HEAD_DIM:(hh + 1) * HEAD_DIM]
                              for gg in range(ng) for hh in range(hb)])

        for gg in range(ng):
            smt_ref[gg] = small_ref[pl.ds(r0 + gg * grp, grp), :].T

        def load_row(base):
            return jnp.stack([smt_ref[gg, pl.ds(base + h0 + hh, 1), :] for gg in range(ng) for hh in range(hb)])

        q_b, k_b, v_b = load(q_ref), load(k_ref), load(v_ref)
        kf = k_b.astype(F32)
        pa = load_row(SM_PA) + dtb
        softplus = jnp.maximum(pa, 0.0) + jnp.log(1.0 + jnp.exp(-jnp.abs(pa)))
        la_row = a_coef * softplus
        beta_row = _sigmoid(load_row(SM_PB))
        nb = la_row.shape[0]
        la_b = jnp.broadcast_to(la_row, (nb, grp, grp))
        g_col = jnp.sum(jnp.where(incl, la_b, 0.0), axis=2, keepdims=True)
        gl_col = jnp.sum(jnp.where(same, la_b, 0.0), axis=2, keepdims=True)
        la_col = jnp.sum(jnp.where(eye, la_b, 0.0), axis=2, keepdims=True)
        beta_col = jnp.sum(jnp.where(eye, jnp.broadcast_to(beta_row, (nb, grp, grp)), 0.0),
                           axis=2, keepdims=True)
        g_row = jnp.sum(jnp.where(incl_t, jnp.broadcast_to(la_col, (nb, grp, grp)), 0.0),
                        axis=1, keepdims=True)
        decay = jnp.where(incl, jnp.exp(jnp.where(incl, g_col - g_row, 0.0)), 0.0)
        kq = _bmm_nt(jnp.concatenate([k_b, q_b], axis=1), k_b)
        a_mat = jnp.where(eye, 0.0, beta_col * kq[:, :grp] * decay)
        t_inv = _unit_lower_inverse(a_mat, ri, ci).astype(BF16)
        rhs = jnp.concatenate([(beta_col * jnp.exp(g_col)) * kf, beta_col * v_b.astype(F32)], axis=2)
        wu = _bmm(t_inv, rhs.astype(BF16)).astype(BF16)
        qk = (kq[:, grp:] * decay).astype(BF16)
        qk_wu = _bmm(qk, wu)
        q_eff = (q_b.astype(F32) * jnp.exp(g_col) - qk_wu[:, :, :HEAD_DIM]).astype(BF16)
        o_loc = qk_wu[:, :, HEAD_DIM:]
        k_end = (kf * jnp.exp(gl_col - g_col)).astype(BF16)
        k_end_t = jnp.stack([k_end[b].T for b in range(nb)])
        chunk_decay = jnp.exp(gl_col)
        masked = jnp.concatenate([jnp.where(col_chunk == cc, k_end_t, jnp.zeros_like(k_end_t))
                                  for cc in range(cpg)], axis=1)
        trans_all = _bmm(masked, wu)
        trans = [trans_all[:, cc * HEAD_DIM:(cc + 1) * HEAD_DIM] for cc in range(cpg)]

        outs = []
        for gg in range(ng):
            ent = slice(gg * hb, (gg + 1) * hb)
            for cc in range(cpg):
                rows = slice(cc * c_len, (cc + 1) * c_len)
                lhs = jnp.concatenate([q_eff[ent, rows], trans[cc][ent, :, :HEAD_DIM].astype(BF16)], axis=1)
                r = _bmm(lhs, state.astype(BF16))
                outs.append(r[:, :c_len] + o_loc[ent, rows])
                state = (chunk_decay[ent, cc * c_len:cc * c_len + 1] * state - r[:, c_len:]
                         + trans[cc][ent, :, HEAD_DIM:])
        o = jnp.concatenate(outs, axis=1)
        o = o * lax.rsqrt(jnp.mean(o * o, axis=-1, keepdims=True) + RMS_EPS) * nw
        o_all = jnp.concatenate([o[hh] for hh in range(hb)], axis=1) * _silu(z_ref[pl.ds(r0, span), :])
        o_ref[pl.ds(r0, span), :] = o_all.astype(o_ref.dtype)
        return state

    lax.fori_loop(0, n_trips, body, jnp.zeros((hb, HEAD_DIM, HEAD_DIM), F32))


def _gdn_chunk(qkv, proj, small, a_log, dt_bias, norm_w, batch, seq):
    t = qkv.shape[0]
    n_groups = seq // GDN_GROUP
    assert n_groups % GDN_NG == 0
    width = GDN_HB * HEAD_DIM
    nhb = GDN_HEADS // GDN_HB
    blk = lambda off: pl.BlockSpec((seq, width), lambda b, h: (b, off + h))
    head_vec = pl.BlockSpec((GDN_HB, 1, HEAD_DIM), lambda b, h: (h, 0, 0))
    kern = functools.partial(_gdn_chunk_kernel, n_trips=n_groups // GDN_NG)
    return pl.pallas_call(
        kern,
        grid=(batch, nhb),
        in_specs=[blk(0), blk(nhb), blk(2 * nhb),
                  pl.BlockSpec((seq, width), lambda b, h: (b, C_Z // width + h)),
                  pl.BlockSpec((seq, SMALL_W), lambda b, h: (b, 0)),
                  head_vec, head_vec,
                  pl.BlockSpec((1, HEAD_DIM), lambda b, h: (0, 0))],
        out_specs=pl.BlockSpec((seq, width), lambda b, h: (b, h)),
        out_shape=jax.ShapeDtypeStruct((t, GDN_W), BF16),
        scratch_shapes=[pltpu.VMEM((GDN_NG, SMALL_W, GDN_GROUP), F32)],
        compiler_params=_cparams(("parallel", "parallel")),
    )(qkv, qkv, qkv, proj, small, a_log, dt_bias, norm_w)


def _cmp_kernel(x_ref, w1_ref, pos_ref, w2_ref, o_ref):
    ns = x_ref.shape[0] // CMP_STRIDE
    both = jnp.zeros((ns, 2 * HEAD_DIM), F32)
    pos_term = jnp.zeros((8, HEAD_DIM), F32)
    for r in range(CMP_STRIDE):
        w_r = w1_ref[0, r]
        both = both + _dot(x_ref[pl.ds(r, ns, stride=CMP_STRIDE), :].astype(BF16), w_r)
        for half in range(2):
            p_row = jnp.broadcast_to(pos_ref[0, half * CMP_STRIDE + r:half * CMP_STRIDE + r + 1, :],
                                     (8, HEAD_DIM)).astype(BF16)
            pos_term = pos_term + _dot(p_row, w_r[:, half * HEAD_DIM:(half + 1) * HEAD_DIM])
    first, second = both[:, :HEAD_DIM], both[:, HEAD_DIM:]
    hid = _silu(first + pltpu.roll(second, ns - 1, axis=0) + pos_term[0:1])
    out = _dot(hid.astype(BF16), w2_ref[0])
    rows = lax.broadcasted_iota(jnp.int32, out.shape, 0)
    o_ref[0, 0, 0] = jnp.where(rows < ns - 1, out, 0.0)


def _nsa_compress(proj, w1cat, pos, w2, batch, seq):
    ns = seq // CMP_STRIDE
    g = NSA_GROUPS
    return pl.pallas_call(
        _cmp_kernel,
        grid=(batch, 2, g),
        in_specs=[pl.BlockSpec((seq, HEAD_DIM),
                               lambda bi, kv, gi: (bi, C_KV // HEAD_DIM + kv * NSA_GROUPS + gi)),
                  pl.BlockSpec((1, CMP_STRIDE, HEAD_DIM, 2 * HEAD_DIM), lambda bi, kv, gi: (kv, 0, 0, 0)),
                  pl.BlockSpec((1, CMP_BLOCK, HEAD_DIM), lambda bi, kv, gi: (kv, 0, 0)),
                  pl.BlockSpec((1, HEAD_DIM, HEAD_DIM), lambda bi, kv, gi: (kv, 0, 0))],
        out_specs=pl.BlockSpec((1, 1, 1, ns, HEAD_DIM), lambda bi, kv, gi: (bi, kv, gi, 0, 0)),
        out_shape=jax.ShapeDtypeStruct((batch, 2, g, ns, HEAD_DIM), F32),
        compiler_params=_cparams(("parallel", "parallel", "parallel")),
    )(proj, w1cat, pos, w2)


NSA_TQ = 256
NSA_KC = 512
NSA_NBP = 128
NSA_MAX_BLOCKS = 64
FEAT_POS_HI = 64
FEAT_POS_LO = 67
SLOPE_TERMS = 3
MASK_BIG = 1e30
LOG2E = 1.4426950408889634


def _masked_softmax(s, valid):
    s = jnp.where(valid, s, NEG_INF)
    m = jnp.max(s, axis=-1, keepdims=True)
    p = jnp.where(valid, jnp.exp(s - m), 0.0)
    l = jnp.sum(p, axis=-1, keepdims=True)
    return p / jnp.where(l > 0.0, l, 1.0)


def _key_features(seq):
    pos = np.arange(seq)
    f = np.zeros((seq, HEAD_DIM), np.float32)
    f[pos, pos // SLC_BLOCK] = 1.0
    f[:, FEAT_POS_HI:FEAT_POS_HI + SLOPE_TERMS] = (SLC_BLOCK * (pos // SLC_BLOCK))[:, None]
    f[:, FEAT_POS_LO:FEAT_POS_LO + SLOPE_TERMS] = (pos % SLC_BLOCK)[:, None]
    return jnp.asarray(f, dtype=BF16)


def _nsa_kernel(q_ref, kc_ref, vc_ref, ks_ref, vs_ref, kw_ref, vw_ref, sm_ref, c2s_ref, feat_ref, o_ref,
                *, tq, seq, kchunk):
    g = pl.program_id(1)
    i = pl.program_id(2)
    t0 = i * tq
    rows = HPG * tq
    ns = kc_ref.shape[-2]
    nbp = NSA_NBP
    top_n = min(SLC_TOPK, seq // SLC_BLOCK)

    q = q_ref[...]
    q32 = jnp.concatenate([q[:, h * HEAD_DIM:(h + 1) * HEAD_DIM] for h in range(HPG)], axis=0)
    q32 = q32 * HEAD_DIM ** -0.5
    qr = q32.astype(BF16)
    qr2 = (q32 * LOG2E).astype(BF16)
    row = lax.broadcasted_iota(jnp.int32, (rows, 1), 0)
    hh = row // tq
    tf = (t0 + row - hh * tq).astype(F32)
    slope = jnp.exp2(-0.5 * (HPG * g + hh + 1).astype(F32))

    kcm = kc_ref[0, 0, 0].astype(BF16)
    vcm = vc_ref[0, 0, 0].astype(BF16)
    jj = lax.broadcasted_iota(jnp.int32, (1, ns), 1)
    dist = tf - (jj * CMP_STRIDE + (CMP_BLOCK - 1)).astype(F32)
    p_cmp = _masked_softmax(_dot_nt(qr, kcm) - slope * dist, dist >= 0.0)
    o_cmp = _dot(p_cmp.astype(BF16), vcm)
    p_sum = p_cmp[0:tq]
    for h in range(1, HPG):
        p_sum = p_sum + p_cmp[h * tq:(h + 1) * tq]
    p_hi, p_lo = _split(p_sum)
    c2s_t = c2s_ref[...]
    imp_t = _dot_nt(c2s_t, p_hi) + _dot_nt(c2s_t, p_lo)

    nbr = NSA_MAX_BLOCKS
    tt = t0 + lax.broadcasted_iota(jnp.int32, (1, tq), 1)
    cur = tt // SLC_BLOCK
    blk = lax.broadcasted_iota(jnp.int32, (nbr, 1), 0)
    blk_f = blk.astype(F32)
    allowed = blk * SLC_BLOCK <= tt
    forced = jnp.logical_or(blk == 0, jnp.logical_or(blk == cur, blk == cur - 1))
    sc = jnp.where(forced, FORCED_SCORE, jnp.where(allowed, imp_t[:nbr], NEG_INF))
    sel_t = jnp.zeros((nbr, tq), F32)
    for _ in range(top_n):
        mx = jnp.max(sc, axis=0, keepdims=True)
        idx = jnp.min(jnp.where(sc == mx, blk_f, float(nbr)), axis=0, keepdims=True)
        pick = blk_f == idx
        sel_t = jnp.where(pick, 1.0, sel_t)
        sc = jnp.where(pick, -jnp.inf, sc)
    sel_t = jnp.where(allowed, sel_t, 0.0)
    sel = jnp.concatenate([sel_t, jnp.zeros((nbp - nbr, tq), F32)], axis=0).T
    sel_bias = jnp.concatenate([(sel - 1.0) * MASK_BIG] * HPG, axis=0)

    lane = lax.broadcasted_iota(jnp.int32, (1, HEAD_DIM), 1)
    sl2 = slope * LOG2E
    terms = []
    rem = sl2
    for _ in range(SLOPE_TERMS):
        term = rem.astype(BF16).astype(F32)
        terms.append(term)
        rem = rem - term
    slope_feat = jnp.zeros((rows, HEAD_DIM), F32)
    for n, term in enumerate(terms):
        hit = jnp.logical_or(lane == FEAT_POS_HI + n, lane == FEAT_POS_LO + n)
        slope_feat = jnp.where(hit, term, slope_feat)
    q_slc = jnp.concatenate([qr2, jnp.where(lane < NSA_MAX_BLOCKS, sel_bias, slope_feat).astype(BF16)], axis=1)
    q_win = jnp.concatenate([qr2, slope_feat.astype(BF16)], axis=1)

    col = lax.broadcasted_iota(jnp.int32, (1, rows), 1)
    t_col = (t0 + col - (col // tq) * tq).astype(F32)
    key_sub = lax.broadcasted_iota(jnp.int32, (kchunk, 1), 0)

    def slc_chunk(c, carry, causal):
        m, l, acc = carry
        k0 = pl.multiple_of(c * kchunk, kchunk)
        kb = jnp.concatenate([ks_ref[pl.ds(k0, kchunk), :].astype(BF16), feat_ref[pl.ds(k0, kchunk), :]], axis=1)
        s = _dot_nt(kb, q_slc)
        if causal:
            s = jnp.where((k0 + key_sub).astype(F32) <= t_col, s, -MASK_BIG)
        m_new = jnp.maximum(m, jnp.max(s, axis=0, keepdims=True))
        alpha = jnp.exp2(m - m_new)
        p = jnp.exp2(s - m_new)
        l = alpha * l + jnp.sum(p, axis=0, keepdims=True)
        v_t = vs_ref[pl.ds(k0, kchunk), :].T.astype(BF16)
        acc = alpha * acc + _dot(v_t, p.astype(BF16))
        return m_new, l, acc

    n_full = t0 // kchunk
    carry = lax.fori_loop(
        0, n_full, lambda c, cr: slc_chunk(c, cr, False),
        (jnp.full((1, rows), NEG_INF, F32), jnp.zeros((1, rows), F32), jnp.zeros((HEAD_DIM, rows), F32)))
    _, l_s, acc_s = slc_chunk(n_full, carry, True)
    o_slc = (acc_s / l_s).T

    wk = WINDOW + tq

    def window_attend(w0, mask_fn):
        kb = jnp.concatenate([kw_ref[pl.ds(w0, wk), :].astype(BF16), feat_ref[pl.ds(w0, wk), :]], axis=1)
        pos = (w0 + lax.broadcasted_iota(jnp.int32, (wk, 1), 0)).astype(F32)
        s = mask_fn(_dot_nt(kb, q_win), pos)
        p = jnp.exp2(s - jnp.max(s, axis=0, keepdims=True))
        v_t = vw_ref[pl.ds(w0, wk), :].T.astype(BF16)
        o_t = _dot(v_t, p.astype(BF16)) / jnp.sum(p, axis=0, keepdims=True)
        return o_t.T

    def banded(s, pos):
        left = jnp.where(pos[:tq] > t_col - float(WINDOW), s[:tq], -MASK_BIG)
        right = jnp.where(pos[wk - tq:] <= t_col, s[wk - tq:], -MASK_BIG)
        return jnp.concatenate([left, s[tq:wk - tq], right], axis=0)

    def head_of_sequence(s, pos):
        return jnp.where(pos <= t_col, s, -MASK_BIG)

    o_win = lax.cond(t0 >= WINDOW,
                     lambda: window_attend(pl.multiple_of(t0 - WINDOW, tq), banded),
                     lambda: window_attend(0, head_of_sequence))

    gt = _sigmoid(sm_ref[...])
    lane_s = lax.broadcasted_iota(jnp.int32, (1, SMALL_W), 1)
    branches = (o_cmp, o_slc, o_win)
    for h in range(HPG):
        out_h = jnp.zeros((tq, HEAD_DIM), F32)
        for br in range(3):
            col = SM_NG + br * NSA_HEADS + g * HPG + h
            gate = jnp.sum(jnp.where(lane_s == col, gt, 0.0), axis=-1, keepdims=True)
            out_h = out_h + gate * branches[br][h * tq:(h + 1) * tq]
        o_ref[:, h * HEAD_DIM:(h + 1) * HEAD_DIM] = out_h.astype(o_ref.dtype)


def _nsa_attend(proj, small, cmp_kv, c2s, feats, batch, seq):
    t = proj.shape[0]
    tq = min(NSA_TQ, seq)
    kchunk = min(NSA_KC, seq)
    nq = seq // tq
    ns = cmp_kv.shape[-2]
    assert seq // SLC_BLOCK <= NSA_MAX_BLOCKS and WINDOW % tq == 0 and kchunk % tq == 0 and WINDOW + tq <= seq
    kv_blk = lambda idx: pl.BlockSpec(
        (seq, HEAD_DIM), lambda b, g, i: (b, (C_KV + idx * NSA_KW) // HEAD_DIM + g))
    cmp_blk = lambda kv: pl.BlockSpec((1, 1, 1, ns, HEAD_DIM), lambda b, g, i: (b, kv, g, 0, 0))
    kern = functools.partial(_nsa_kernel, tq=tq, seq=seq, kchunk=kchunk)
    return pl.pallas_call(
        kern,
        grid=(batch, NSA_GROUPS, nq),
        in_specs=[pl.BlockSpec((tq, NSA_KW), lambda b, g, i: (b * nq + i, C_NQ // NSA_KW + g)),
                  cmp_blk(0), cmp_blk(1),
                  kv_blk(2), kv_blk(3), kv_blk(4), kv_blk(5),
                  pl.BlockSpec((tq, SMALL_W), lambda b, g, i: (b * nq + i, 0)),
                  pl.BlockSpec((NSA_NBP, ns), lambda b, g, i: (0, 0)),
                  pl.BlockSpec((seq, HEAD_DIM), lambda b, g, i: (0, 0))],
        out_specs=pl.BlockSpec((tq, NSA_KW), lambda b, g, i: (b * nq + i, g)),
        out_shape=jax.ShapeDtypeStruct((t, NSA_QW), BF16),
        compiler_params=_cparams(("parallel", "parallel", "arbitrary")),
    )(proj, cmp_kv, cmp_kv, proj, proj, proj, proj, small, c2s, feats)


def _readout_kernel(oa_ref, ob_ref, wa_ref, wb_ref, ma_ref, mb_ref, y_ref):
    ya = _dot(oa_ref[...], wa_ref[...])
    yb = _dot(ob_ref[...], wb_ref[...])
    y_ref[...] = (_sigmoid(ma_ref[...]) * ya + _sigmoid(mb_ref[...]) * yb).astype(y_ref.dtype)


def _readout(o_a, o_b, w_a, w_b, layer, proj, d_model, tm, tn):
    t = o_a.shape[0]
    tm, tn = min(tm, t), min(tn, d_model)
    c_mb = C_MA + d_model
    return pl.pallas_call(
        _readout_kernel,
        grid=(t // tm, d_model // tn),
        in_specs=[pl.BlockSpec((tm, GDN_W), lambda i, j: (i, 0)),
                  pl.BlockSpec((tm, NSA_QW), lambda i, j: (i, 0)),
                  pl.BlockSpec((None, GDN_W, tn), lambda i, j: (layer, 0, j)),
                  pl.BlockSpec((None, NSA_QW, tn), lambda i, j: (layer, 0, j)),
                  pl.BlockSpec((tm, tn), lambda i, j: (i, C_MA // tn + j)),
                  pl.BlockSpec((tm, tn), lambda i, j: (i, c_mb // tn + j))],
        out_specs=pl.BlockSpec((tm, tn), lambda i, j: (i, j)),
        out_shape=jax.ShapeDtypeStruct((t, d_model), BF16),
        compiler_params=_cparams(("parallel", "parallel")),
    )(o_a, o_b, w_a, w_b, proj, proj)


def _cmp_to_slc(seq):
    ns = seq // CMP_STRIDE
    nb = seq // SLC_BLOCK
    start = np.arange(ns) * CMP_STRIDE
    end = start + CMP_BLOCK - 1
    s0 = np.arange(nb) * SLC_BLOCK
    m = (end[:, None] >= s0[None, :]) & (start[:, None] <= s0[None, :] + SLC_BLOCK - 1)
    m[ns - 1] = False
    out = np.zeros((NSA_NBP, ns), np.float32)
    out[:nb] = m.T
    return jnp.asarray(out, dtype=BF16)


PERM_TR = 1024
PERM_TAIL = 128


def _permute_kernel(a_ref, b_ref, o_ref, *, ranges):
    j = pl.program_id(1)
    x = jnp.concatenate([a_ref[...], b_ref[...]], axis=1)
    for lo, hi, shift in ranges:
        @pl.when(jnp.logical_and(j >= lo, j < hi))
        def _():
            o_ref[...] = x[:, shift:shift + PROJ_TN].astype(o_ref.dtype)


def _permute_w_in(w_in, d_model):
    depth, d, _ = w_in.shape
    g0 = 4 * GDN_W
    s0 = g0 + 2 * GDN_HEADS
    s1 = s0 + NSA_QW + 6 * NSA_KW
    s2 = s1 + 3 * NSA_HEADS
    n_main = C_MA + 2 * d_model
    bounds = (0, g0 // PROJ_TN, C_MA // PROJ_TN, n_main // PROJ_TN)
    shifts = (0, s0 - g0, s2 - C_MA)
    assert g0 % PROJ_TN == 0 and C_MA % PROJ_TN == 0 and n_main % PROJ_TN == 0 and max(shifts) <= PERM_TAIL
    ranges = tuple((bounds[r], bounds[r + 1], shifts[r]) for r in range(3))
    tr = min(PERM_TR, d)
    tail_per_blk = PROJ_TN // PERM_TAIL
    w_bf = w_in.astype(BF16)
    main = pl.pallas_call(
        functools.partial(_permute_kernel, ranges=ranges),
        grid=(depth, n_main // PROJ_TN, d // tr),
        in_specs=[pl.BlockSpec((None, tr, PROJ_TN), lambda l, j, r: (l, r, j)),
                  pl.BlockSpec((None, tr, PERM_TAIL), lambda l, j, r: (l, r, (j + 1) * tail_per_blk))],
        out_specs=pl.BlockSpec((None, tr, PROJ_TN), lambda l, j, r: (l, r, j)),
        out_shape=jax.ShapeDtypeStruct((depth, d, n_main), BF16),
        compiler_params=_cparams(("parallel", "parallel", "parallel")),
    )(w_bf, w_bf)
    small = jnp.concatenate([w_bf[..., g0:s0], w_bf[..., s1:s2],
                             jnp.zeros((depth, d, SMALL_W - (s0 - g0) - (s2 - s1)), BF16)], axis=-1)
    return main, small


def _mixer(u, layer, proj_w, small_w, conv_w, a_log, dt_bias, norm_w, cmp_pos, cmp_w1, cmp_w2, w_read_a, w_read_b,
           w_out, c2s, feats, batch, seq, d_model):
    proj = _matmul(u, proj_w, layer, tm=1024, tn=PROJ_TN)
    small = _matmul(u, small_w, layer, tm=1024, tn=SMALL_W)

    qkv = _gdn_prep(proj, conv_w, seq, ts=512)
    bcast = lambda v: jnp.broadcast_to(v.reshape(GDN_HEADS, 1, 1), (GDN_HEADS, 1, HEAD_DIM))
    o_a = _gdn_chunk(qkv, proj, small, bcast(a_log), bcast(dt_bias), norm_w.reshape(1, HEAD_DIM), batch, seq)

    w1cat = jnp.concatenate([cmp_w1[:, :CMP_STRIDE], cmp_w1[:, CMP_STRIDE:]], axis=-1).astype(BF16)
    cmp_kv = _nsa_compress(proj, w1cat, cmp_pos, cmp_w2.astype(BF16), batch, seq)
    o_b = _nsa_attend(proj, small, cmp_kv, c2s, feats, batch, seq)

    y = _readout(o_a, o_b, w_read_a, w_read_b, layer, proj, d_model, tm=1024, tn=512)
    return _matmul(y, w_out, layer, tm=1024, tn=512)


def kernel(x, c, ada_w, ada_b, ada_table, w_in, gdn_conv_w, gdn_a_log, gdn_dt_bias, gdn_norm_w,
           cmp_pos, cmp_w1, cmp_w2, w_read_a, w_read_b, w_out, ln1_g, ln1_b,
           mlp_w1, mlp_b1, mlp_w2, mlp_b2, ln2_g, ln2_b):
    batch, seq, d_model = x.shape
    depth = w_in.shape[0]
    d_ff = mlp_w1.shape[-1]
    t = batch * seq
    alpha = (2.0 * depth) ** 0.25
    c2s = _cmp_to_slc(seq)
    feats = _key_features(seq)

    proj_w, small_w = _permute_w_in(w_in, d_model)
    w_read_a, w_read_b, w_out = (w.astype(BF16) for w in (w_read_a, w_read_b, w_out))
    mlp_w1, mlp_w2 = mlp_w1.astype(BF16), mlp_w2.astype(BF16)

    mod = _ada_projection(c, ada_w, ada_b)

    x2 = x.reshape(t, d_model)
    u = None
    for l in range(depth):
        m = (mod + ada_table[l].reshape(1, -1)).reshape(batch, N_ADA, 1, d_model)
        shift1, scale1, gate1, shift2, scale2, gate2 = (m[:, j] for j in range(N_ADA))
        if u is None:
            u = _modulate(x2, scale1, shift1, seq, ts=512)
        h = _mixer(u, l, proj_w, small_w, gdn_conv_w[l], gdn_a_log[l], gdn_dt_bias[l],
                   gdn_norm_w[l], cmp_pos[l], cmp_w1[l], cmp_w2[l], w_read_a, w_read_b, w_out,
                   c2s, feats, batch, seq, d_model)
        x2, u = _deepnorm_ln(x2, h, gate1, ln1_g[l], ln1_b[l], seq, 256, alpha, scale2, shift2)

        tn1 = min(1024, d_ff)
        act = _matmul(u, mlp_w1, l, tm=1024, tn=tn1, out_dtype=BF16,
                      extras=(mlp_b1[l].reshape(1, d_ff),),
                      extra_specs=(pl.BlockSpec((1, tn1), lambda i, j, k: (0, j)),),
                      epilogue=lambda acc, b: jnp.square(jnp.maximum(acc + b, 0.0)))
        tn2 = min(1024, d_model)
        h = _matmul(act, mlp_w2, l, tm=1024, tn=tn2, tk=4096,
                    extras=(mlp_b2[l].reshape(1, d_model),),
                    extra_specs=(pl.BlockSpec((1, tn2), lambda i, j, k: (0, j)),),
                    epilogue=lambda acc, b: acc + b)
        if l + 1 < depth:
            m_next = (mod + ada_table[l + 1].reshape(1, -1)).reshape(batch, N_ADA, 1, d_model)
            x2, u = _deepnorm_ln(x2, h, gate2, ln2_g[l], ln2_b[l], seq, 256, alpha,
                                 m_next[:, 1], m_next[:, 0])
        else:
            x2, _ = _deepnorm_ln(x2, h, gate2, ln2_g[l], ln2_b[l], seq, 256, alpha)
    return x2.reshape(batch, seq, d_model)
```

```python
import functools

import numpy as np
import jax
import jax.numpy as jnp
from jax import lax
from jax.experimental import pallas as pl
from jax.experimental.pallas import tpu as pltpu

F32 = jnp.float32
BF16 = jnp.bfloat16

HEAD_DIM = 128
GDN_HEADS = 16
GDN_CONV = 4
GDN_CHUNK = 64
NSA_HEADS = 16
NSA_GROUPS = 4
HPG = NSA_HEADS // NSA_GROUPS
CMP_BLOCK = 32
CMP_STRIDE = 16
SLC_BLOCK = 64
SLC_TOPK = 16
WINDOW = 512
N_ADA = 6
LN_EPS = 1e-5
RMS_EPS = 1e-6
NEG_INF = -1e30
FORCED_SCORE = 1e6

GDN_W = GDN_HEADS * HEAD_DIM
NSA_QW = NSA_HEADS * HEAD_DIM
NSA_KW = NSA_GROUPS * HEAD_DIM

C_QKV = 0
C_Z = 3 * GDN_W
C_NQ = C_Z + GDN_W
C_KV = C_NQ + NSA_QW
C_MA = C_KV + 6 * NSA_KW
SMALL_W = 128
SM_PB, SM_PA, SM_NG = 0, GDN_HEADS, 2 * GDN_HEADS
PROJ_TN = 512

VMEM_LIMIT = 56 * 1024 * 1024


def _cparams(sem):
    return pltpu.CompilerParams(dimension_semantics=sem, vmem_limit_bytes=VMEM_LIMIT)


def _sigmoid(x):
    return 1.0 / (1.0 + jnp.exp(-x))


def _silu(x):
    return x * _sigmoid(x)


def _dot(a, b):
    return jnp.dot(a, b, preferred_element_type=F32)


def _dot_nt(a, b):
    return lax.dot_general(a, b, (((1,), (1,)), ((), ())), preferred_element_type=F32)


def _split(a):
    hi = a.astype(BF16)
    lo = (a - hi.astype(F32)).astype(BF16)
    return hi, lo


def _mm_kernel(*refs, nk, n_extra, epilogue):
    a_ref, b_ref = refs[0], refs[1]
    extra = refs[2:2 + n_extra]
    o_ref = refs[2 + n_extra]
    if nk == 1:
        acc = _dot(a_ref[...], b_ref[...])
        o_ref[...] = epilogue(acc, *[e[...] for e in extra]).astype(o_ref.dtype)
    else:
        acc_ref = refs[3 + n_extra]
        k = pl.program_id(2)

        @pl.when(k == 0)
        def _():
            acc_ref[...] = jnp.zeros_like(acc_ref)

        acc_ref[...] += _dot(a_ref[...], b_ref[...])

        @pl.when(k == nk - 1)
        def _():
            o_ref[...] = epilogue(acc_ref[...], *[e[...] for e in extra]).astype(o_ref.dtype)


def _matmul(a, b, layer, *, tm, tn, tk=None, out_dtype=F32, extras=(), extra_specs=(), epilogue=None):
    m, kdim = a.shape
    n = b.shape[2]
    tm, tn = min(tm, m), min(tn, n)
    tk = kdim if tk is None else min(tk, kdim)
    nk = kdim // tk
    if epilogue is None:
        epilogue = lambda acc: acc
    kern = functools.partial(_mm_kernel, nk=nk, n_extra=len(extras), epilogue=epilogue)
    in_specs = [pl.BlockSpec((tm, tk), lambda i, j, k: (i, k)),
                pl.BlockSpec((None, tk, tn), lambda i, j, k: (layer, k, j))] + list(extra_specs)
    scratch = [] if nk == 1 else [pltpu.VMEM((tm, tn), F32)]
    return pl.pallas_call(
        kern,
        grid=(m // tm, n // tn, nk),
        in_specs=in_specs,
        out_specs=pl.BlockSpec((tm, tn), lambda i, j, k: (i, j)),
        out_shape=jax.ShapeDtypeStruct((m, n), out_dtype),
        scratch_shapes=scratch,
        compiler_params=_cparams(("parallel", "parallel", "arbitrary")),
    )(a, b, *extras)


ADA_ROWS = 8
ADA_TK = 128


def _ada_kernel(c_ref, w_ref, b_ref, o_ref):
    @pl.when(pl.program_id(0) == 0)
    def _():
        o_ref[...] = jnp.broadcast_to(b_ref[...], o_ref.shape)

    o_ref[...] += _dot(_silu(c_ref[...]).astype(BF16), w_ref[...].astype(BF16))


def _ada_projection(c, ada_w, ada_b):
    batch, d = c.shape
    n = ada_w.shape[1]
    tk = min(ADA_TK, d)
    assert batch <= ADA_ROWS
    c_pad = jnp.zeros((ADA_ROWS, d), F32).at[:batch].set(c)
    out = pl.pallas_call(
        _ada_kernel,
        grid=(d // tk,),
        in_specs=[pl.BlockSpec((ADA_ROWS, tk), lambda k: (0, k)),
                  pl.BlockSpec((tk, n), lambda k: (k, 0)),
                  pl.BlockSpec((1, n), lambda k: (0, 0))],
        out_specs=pl.BlockSpec((ADA_ROWS, n), lambda k: (0, 0)),
        out_shape=jax.ShapeDtypeStruct((ADA_ROWS, n), F32),
        compiler_params=_cparams(("arbitrary",)),
    )(c_pad, ada_w, ada_b.reshape(1, n))
    return out[:batch]


def _modulate_kernel(x_ref, sc_ref, sh_ref, u_ref):
    u_ref[...] = (x_ref[...] * (1.0 + sc_ref[0]) + sh_ref[0]).astype(u_ref.dtype)


def _modulate(x2, scale, shift, seq, ts):
    t, d = x2.shape
    ts = min(ts, seq)
    per_b = lambda i: ((i * ts) // seq, 0, 0)
    return pl.pallas_call(
        _modulate_kernel,
        grid=(t // ts,),
        in_specs=[pl.BlockSpec((ts, d), lambda i: (i, 0)),
                  pl.BlockSpec((1, 1, d), per_b),
                  pl.BlockSpec((1, 1, d), per_b)],
        out_specs=pl.BlockSpec((ts, d), lambda i: (i, 0)),
        out_shape=jax.ShapeDtypeStruct((t, d), BF16),
        compiler_params=_cparams(("parallel",)),
    )(x2, scale, shift)


def _ln_kernel(*refs, alpha, emit_u):
    if emit_u:
        x_ref, h_ref, gate_ref, g_ref, b_ref, sc_ref, sh_ref, xo_ref, uo_ref = refs
    else:
        x_ref, h_ref, gate_ref, g_ref, b_ref, xo_ref = refs
    v = alpha * x_ref[...] + (1.0 + gate_ref[0]) * h_ref[...]
    mu = jnp.mean(v, axis=-1, keepdims=True)
    vc = v - mu
    var = jnp.mean(vc * vc, axis=-1, keepdims=True)
    xn = vc * lax.rsqrt(var + LN_EPS) * g_ref[...] + b_ref[...]
    xo_ref[...] = xn
    if emit_u:
        uo_ref[...] = (xn * (1.0 + sc_ref[0]) + sh_ref[0]).astype(uo_ref.dtype)


def _deepnorm_ln(x2, h2, gate, ln_g, ln_b, seq, ts, alpha, next_scale=None, next_shift=None):
    t, d = x2.shape
    ts = min(ts, seq)
    emit_u = next_scale is not None
    per_b = lambda i: ((i * ts) // seq, 0, 0)
    row = pl.BlockSpec((ts, d), lambda i: (i, 0))
    vec_b = pl.BlockSpec((1, 1, d), per_b)
    vec = pl.BlockSpec((1, d), lambda i: (0, 0))
    in_specs = [row, row, vec_b, vec, vec]
    args = [x2, h2, gate, ln_g.reshape(1, d), ln_b.reshape(1, d)]
    out_specs = [row]
    out_shape = [jax.ShapeDtypeStruct((t, d), F32)]
    if emit_u:
        in_specs += [vec_b, vec_b]
        args += [next_scale, next_shift]
        out_specs.append(row)
        out_shape.append(jax.ShapeDtypeStruct((t, d), BF16))
    res = pl.pallas_call(
        functools.partial(_ln_kernel, alpha=alpha, emit_u=emit_u),
        grid=(t // ts,),
        in_specs=in_specs,
        out_specs=out_specs,
        out_shape=out_shape,
        compiler_params=_cparams(("parallel",)),
    )(*args)
    return (res[0], res[1]) if emit_u else (res[0], None)


GDN_PREP_TN = 512


def _gdn_prep_kernel(cur_ref, halo_ref, w_ref, o_ref, buf_ref, *, ts, seq):
    i = pl.program_id(0)
    j = pl.program_id(1)
    first = (i * ts) % seq == 0
    buf_ref[0:8, :] = jnp.where(first, 0.0, halo_ref[...])
    buf_ref[8:8 + ts, :] = cur_ref[...]
    w = w_ref[...]
    y = w[3:4] * buf_ref[8:8 + ts, :]
    for tap in range(GDN_CONV - 1):
        y = y + w[tap:tap + 1] * buf_ref[5 + tap:5 + tap + ts, :]
    y = _silu(y)
    heads_per_blk = GDN_PREP_TN // HEAD_DIM
    is_q = j < GDN_W // GDN_PREP_TN
    is_qk = j < 2 * GDN_W // GDN_PREP_TN
    for h in range(heads_per_blk):
        yh = y[:, h * HEAD_DIM:(h + 1) * HEAD_DIM]
        r = lax.rsqrt(jnp.sum(yh * yh, axis=-1, keepdims=True) + RMS_EPS)
        scale = jnp.where(is_q, r * HEAD_DIM ** -0.5, jnp.where(is_qk, r, 1.0))
        o_ref[:, h * HEAD_DIM:(h + 1) * HEAD_DIM] = (yh * scale).astype(o_ref.dtype)


def _gdn_prep(proj, conv_w, seq, ts):
    t = proj.shape[0]
    ts = min(ts, seq)
    tn = GDN_PREP_TN
    kern = functools.partial(_gdn_prep_kernel, ts=ts, seq=seq)
    return pl.pallas_call(
        kern,
        grid=(t // ts, 3 * GDN_W // tn),
        in_specs=[pl.BlockSpec((ts, tn), lambda i, j: (i, j)),
                  pl.BlockSpec((8, tn), lambda i, j: (jnp.maximum(i * (ts // 8) - 1, 0), j)),
                  pl.BlockSpec((GDN_CONV, tn), lambda i, j: (0, j))],
        out_specs=pl.BlockSpec((ts, tn), lambda i, j: (i, j)),
        out_shape=jax.ShapeDtypeStruct((t, 3 * GDN_W), BF16),
        scratch_shapes=[pltpu.VMEM((ts + 8, tn), F32)],
        compiler_params=_cparams(("parallel", "parallel")),
    )(proj, proj, conv_w)


GDN_GROUP = 256
GDN_HB = 2
GDN_NG = 2


def _bmm(a, b):
    return lax.dot_general(a, b, (((2,), (1,)), ((0,), (0,))), preferred_element_type=F32)


def _bmm_nt(a, b):
    return lax.dot_general(a, b, (((2,), (2,)), ((0,), (0,))), preferred_element_type=F32)


def _unit_lower_inverse(a, ri, ci):
    grp = a.shape[-1]
    eye = (ri == ci).astype(F32)
    blk16 = (ri // 16) == (ci // 16)
    blk32 = (ri // 32) == (ci // 32)
    d = jnp.where(blk16, a, 0.0)
    x = eye - d
    d_bf = d.astype(BF16)
    p = _bmm(d_bf, d_bf)
    for _ in range(2):
        p_bf = p.astype(BF16)
        r = _bmm(jnp.concatenate([x.astype(BF16), p_bf], axis=1), p_bf)
        x = x + r[:, :grp]
        p = r[:, grp:]
    x = x + _bmm(x.astype(BF16), p.astype(BF16))
    for e in (jnp.where(blk16, 0.0, jnp.where(blk32, a, 0.0)), jnp.where(blk32, 0.0, a)):
        x_bf = x.astype(BF16)
        x = x - _bmm(_bmm(x_bf, e.astype(BF16)).astype(BF16), x_bf)
    return x


def _gdn_chunk_kernel(q_ref, k_ref, v_ref, z_ref, small_ref, alog_ref, dtb_ref, nw_ref, o_ref, smt_ref,
                      *, n_trips):
    grp, c_len, hb, ng = GDN_GROUP, GDN_CHUNK, GDN_HB, GDN_NG
    cpg = grp // c_len
    span = ng * grp
    ri = lax.broadcasted_iota(jnp.int32, (grp, grp), 0)
    ci = lax.broadcasted_iota(jnp.int32, (grp, grp), 1)
    same = (ri // c_len) == (ci // c_len)
    eye = ri == ci
    incl = jnp.logical_and(same, ri >= ci)
    incl_t = jnp.logical_and(same, ci >= ri)
    col_chunk = lax.broadcasted_iota(jnp.int32, (HEAD_DIM, grp), 1) // c_len
    nw = nw_ref[...]
    h0 = pl.program_id(1) * hb
    a_coef = jnp.stack([-jnp.exp(alog_ref[hh][:, 0:1]) for _ in range(ng) for hh in range(hb)])
    dtb = jnp.stack([dtb_ref[hh][:, 0:1] for _ in range(ng) for hh in range(hb)])

    def body(it, state):
        r0 = pl.multiple_of(it * span, span)

        def load(ref):
            return jnp.stack([ref[pl.ds(r0 + gg * grp, grp), hh * HEAD_DIM:(hh + 1) * HEAD_DIM]
                              for gg in range(ng) for hh in range(hb)])

        for gg in range(ng):
            smt_ref[gg] = small_ref[pl.ds(r0 + gg * grp, grp), :].T

        def load_row(base):
            return jnp.stack([smt_ref[gg, pl.ds(base + h0 + hh, 1), :] for gg in range(ng) for hh in range(hb)])

        q_b, k_b, v_b = load(q_ref), load(k_ref), load(v_ref)
        kf = k_b.astype(F32)
        pa = load_row(SM_PA) + dtb
        softplus = jnp.maximum(pa, 0.0) + jnp.log(1.0 + jnp.exp(-jnp.abs(pa)))
        la_row = a_coef * softplus
        beta_row = _sigmoid(load_row(SM_PB))
        nb = la_row.shape[0]
        la_b = jnp.broadcast_to(la_row, (nb, grp, grp))
        g_col = jnp.sum(jnp.where(incl, la_b, 0.0), axis=2, keepdims=True)
        gl_col = jnp.sum(jnp.where(same, la_b, 0.0), axis=2, keepdims=True)
        la_col = jnp.sum(jnp.where(eye, la_b, 0.0), axis=2, keepdims=True)
        beta_col = jnp.sum(jnp.where(eye, jnp.broadcast_to(beta_row, (nb, grp, grp)), 0.0),
                           axis=2, keepdims=True)
        g_row = jnp.sum(jnp.where(incl_t, jnp.broadcast_to(la_col, (nb, grp, grp)), 0.0),
                        axis=1, keepdims=True)
        decay = jnp.where(incl, jnp.exp(jnp.where(incl, g_col - g_row, 0.0)), 0.0)
        kq = _bmm_nt(jnp.concatenate([k_b, q_b], axis=1), k_b)
        a_mat = jnp.where(eye, 0.0, beta_col * kq[:, :grp] * decay)
        t_inv = _unit_lower_inverse(a_mat, ri, ci).astype(BF16)
        rhs = jnp.concatenate([(beta_col * jnp.exp(g_col)) * kf, beta_col * v_b.astype(F32)], axis=2)
        wu = _bmm(t_inv, rhs.astype(BF16)).astype(BF16)
        qk = (kq[:, grp:] * decay).astype(BF16)
        qk_wu = _bmm(qk, wu)
        q_eff = (q_b.astype(F32) * jnp.exp(g_col) - qk_wu[:, :, :HEAD_DIM]).astype(BF16)
        o_loc = qk_wu[:, :, HEAD_DIM:]
        k_end = (kf * jnp.exp(gl_col - g_col)).astype(BF16)
        k_end_t = jnp.stack([k_end[b].T for b in range(nb)])
        chunk_decay = jnp.exp(gl_col)
        masked = jnp.concatenate([jnp.where(col_chunk == cc, k_end_t, jnp.zeros_like(k_end_t))
                                  for cc in range(cpg)], axis=1)
        trans_all = _bmm(masked, wu)
        trans = [trans_all[:, cc * HEAD_DIM:(cc + 1) * HEAD_DIM] for cc in range(cpg)]

        outs = []
        for gg in range(ng):
            ent = slice(gg * hb, (gg + 1) * hb)
            for cc in range(cpg):
                rows = slice(cc * c_len, (cc + 1) * c_len)
                lhs = jnp.concatenate([q_eff[ent, rows], trans[cc][ent, :, :HEAD_DIM].astype(BF16)], axis=1)
                r = _bmm(lhs, state.astype(BF16))
                outs.append(r[:, :c_len] + o_loc[ent, rows])
                state = (chunk_decay[ent, cc * c_len:cc * c_len + 1] * state - r[:, c_len:]
                         + trans[cc][ent, :, HEAD_DIM:])
        o = jnp.concatenate(outs, axis=1)
        o = o * lax.rsqrt(jnp.mean(o * o, axis=-1, keepdims=True) + RMS_EPS) * nw
        o_all = jnp.concatenate([o[hh] for hh in range(hb)], axis=1) * _silu(z_ref[pl.ds(r0, span), :])
        o_ref[pl.ds(r0, span), :] = o_all.astype(o_ref.dtype)
        return state

    lax.fori_loop(0, n_trips, body, jnp.zeros((hb, HEAD_DIM, HEAD_DIM), F32))


def _gdn_chunk(qkv, proj, small, a_log, dt_bias, norm_w, batch, seq):
    t = qkv.shape[0]
    n_groups = seq // GDN_GROUP
    assert n_groups % GDN_NG == 0
    width = GDN_HB * HEAD_DIM
    nhb = GDN_HEADS // GDN_HB
    blk = lambda off: pl.BlockSpec((seq, width), lambda b, h: (b, off + h))
    head_vec = pl.BlockSpec((GDN_HB, 1, HEAD_DIM), lambda b, h: (h, 0, 0))
    kern = functools.partial(_gdn_chunk_kernel, n_trips=n_groups // GDN_NG)
    return pl.pallas_call(
        kern,
        grid=(batch, nhb),
        in_specs=[blk(0), blk(nhb), blk(2 * nhb),
                  pl.BlockSpec((seq, width), lambda b, h: (b, C_Z // width + h)),
                  pl.BlockSpec((seq, SMALL_W), lambda b, h: (b, 0)),
                  head_vec, head_vec,
                  pl.BlockSpec((1, HEAD_DIM), lambda b, h: (0, 0))],
        out_specs=pl.BlockSpec((seq, width), lambda b, h: (b, h)),
        out_shape=jax.ShapeDtypeStruct((t, GDN_W), BF16),
        scratch_shapes=[pltpu.VMEM((GDN_NG, SMALL_W, GDN_GROUP), F32)],
        compiler_params=_cparams(("parallel", "parallel")),
    )(qkv, qkv, qkv, proj, small, a_log, dt_bias, norm_w)


def _cmp_kernel(x_ref, w1_ref, pos_ref, w2_ref, o_ref):
    ns = x_ref.shape[0] // CMP_STRIDE
    both = jnp.zeros((ns, 2 * HEAD_DIM), F32)
    pos_term = jnp.zeros((8, HEAD_DIM), F32)
    for r in range(CMP_STRIDE):
        w_r = w1_ref[0, r]
        both = both + _dot(x_ref[pl.ds(r, ns, stride=CMP_STRIDE), :].astype(BF16), w_r)
        for half in range(2):
            p_row = jnp.broadcast_to(pos_ref[0, half * CMP_STRIDE + r:half * CMP_STRIDE + r + 1, :],
                                     (8, HEAD_DIM)).astype(BF16)
            pos_term = pos_term + _dot(p_row, w_r[:, half * HEAD_DIM:(half + 1) * HEAD_DIM])
    first, second = both[:, :HEAD_DIM], both[:, HEAD_DIM:]
    hid = _silu(first + pltpu.roll(second, ns - 1, axis=0) + pos_term[0:1])
    out = _dot(hid.astype(BF16), w2_ref[0])
    rows = lax.broadcasted_iota(jnp.int32, out.shape, 0)
    o_ref[0, 0, 0] = jnp.where(rows < ns - 1, out, 0.0)


def _nsa_compress(proj, w1cat, pos, w2, batch, seq):
    ns = seq // CMP_STRIDE
    g = NSA_GROUPS
    return pl.pallas_call(
        _cmp_kernel,
        grid=(batch, 2, g),
        in_specs=[pl.BlockSpec((seq, HEAD_DIM),
                               lambda bi, kv, gi: (bi, C_KV // HEAD_DIM + kv * NSA_GROUPS + gi)),
                  pl.BlockSpec((1, CMP_STRIDE, HEAD_DIM, 2 * HEAD_DIM), lambda bi, kv, gi: (kv, 0, 0, 0)),
                  pl.BlockSpec((1, CMP_BLOCK, HEAD_DIM), lambda bi, kv, gi: (kv, 0, 0)),
                  pl.BlockSpec((1, HEAD_DIM, HEAD_DIM), lambda bi, kv, gi: (kv, 0, 0))],
        out_specs=pl.BlockSpec((1, 1, 1, ns, HEAD_DIM), lambda bi, kv, gi: (bi, kv, gi, 0, 0)),
        out_shape=jax.ShapeDtypeStruct((batch, 2, g, ns, HEAD_DIM), F32),
        compiler_params=_cparams(("parallel", "parallel", "parallel")),
    )(proj, w1cat, pos, w2)


NSA_TQ = 256
NSA_KC = 512
NSA_NBP = 128
NSA_MAX_BLOCKS = 64
FEAT_POS_HI = 64
FEAT_POS_LO = 67
SLOPE_TERMS = 3
MASK_BIG = 1e30
LOG2E = 1.4426950408889634


def _masked_softmax(s, valid):
    s = jnp.where(valid, s, NEG_INF)
    m = jnp.max(s, axis=-1, keepdims=True)
    p = jnp.where(valid, jnp.exp(s - m), 0.0)
    l = jnp.sum(p, axis=-1, keepdims=True)
    return p / jnp.where(l > 0.0, l, 1.0)


def _key_features(seq):
    pos = np.arange(seq)
    f = np.zeros((seq, HEAD_DIM), np.float32)
    f[pos, pos // SLC_BLOCK] = 1.0
    f[:, FEAT_POS_HI:FEAT_POS_HI + SLOPE_TERMS] = (SLC_BLOCK * (pos // SLC_BLOCK))[:, None]
    f[:, FEAT_POS_LO:FEAT_POS_LO + SLOPE_TERMS] = (pos % SLC_BLOCK)[:, None]
    return jnp.asarray(f, dtype=BF16)


def _nsa_kernel(q_ref, kc_ref, vc_ref, ks_ref, vs_ref, kw_ref, vw_ref, sm_ref, c2s_ref, feat_ref, o_ref,
                *, tq, seq, kchunk):
    g = pl.program_id(1)
    i = pl.program_id(2)
    t0 = i * tq
    rows = HPG * tq
    ns = kc_ref.shape[-2]
    nbp = NSA_NBP
    top_n = min(SLC_TOPK, seq // SLC_BLOCK)

    q = q_ref[...]
    q32 = jnp.concatenate([q[:, h * HEAD_DIM:(h + 1) * HEAD_DIM] for h in range(HPG)], axis=0)
    q32 = q32 * HEAD_DIM ** -0.5
    qr = q32.astype(BF16)
    qr2 = (q32 * LOG2E).astype(BF16)
    row = lax.broadcasted_iota(jnp.int32, (rows, 1), 0)
    hh = row // tq
    tf = (t0 + row - hh * tq).astype(F32)
    slope = jnp.exp2(-0.5 * (HPG * g + hh + 1).astype(F32))

    kcm = kc_ref[0, 0, 0].astype(BF16)
    vcm = vc_ref[0, 0, 0].astype(BF16)
    jj = lax.broadcasted_iota(jnp.int32, (1, ns), 1)
    dist = tf - (jj * CMP_STRIDE + (CMP_BLOCK - 1)).astype(F32)
    p_cmp = _masked_softmax(_dot_nt(qr, kcm) - slope * dist, dist >= 0.0)
    o_cmp = _dot(p_cmp.astype(BF16), vcm)
    p_sum = p_cmp[0:tq]
    for h in range(1, HPG):
        p_sum = p_sum + p_cmp[h * tq:(h + 1) * tq]
    p_hi, p_lo = _split(p_sum)
    c2s_t = c2s_ref[...]
    imp_t = _dot_nt(c2s_t, p_hi) + _dot_nt(c2s_t, p_lo)

    nbr = NSA_MAX_BLOCKS
    tt = t0 + lax.broadcasted_iota(jnp.int32, (1, tq), 1)
    cur = tt // SLC_BLOCK
    blk = lax.broadcasted_iota(jnp.int32, (nbr, 1), 0)
    blk_f = blk.astype(F32)
    allowed = blk * SLC_BLOCK <= tt
    forced = jnp.logical_or(blk == 0, jnp.logical_or(blk == cur, blk == cur - 1))
    sc = jnp.where(forced, FORCED_SCORE, jnp.where(allowed, imp_t[:nbr], NEG_INF))
    sel_t = jnp.zeros((nbr, tq), F32)
    for _ in range(top_n):
        mx = jnp.max(sc, axis=0, keepdims=True)
        idx = jnp.min(jnp.where(sc == mx, blk_f, float(nbr)), axis=0, keepdims=True)
        pick = blk_f == idx
        sel_t = jnp.where(pick, 1.0, sel_t)
        sc = jnp.where(pick, -jnp.inf, sc)
    sel_t = jnp.where(allowed, sel_t, 0.0)
    sel = jnp.concatenate([sel_t, jnp.zeros((nbp - nbr, tq), F32)], axis=0).T
    sel_bias = jnp.concatenate([(sel - 1.0) * MASK_BIG] * HPG, axis=0)

    lane = lax.broadcasted_iota(jnp.int32, (1, HEAD_DIM), 1)
    sl2 = slope * LOG2E
    terms = []
    rem = sl2
    for _ in range(SLOPE_TERMS):
        term = rem.astype(BF16).astype(F32)
        terms.append(term)
        rem = rem - term
    slope_feat = jnp.zeros((rows, HEAD_DIM), F32)
    for n, term in enumerate(terms):
        hit = jnp.logical_or(lane == FEAT_POS_HI + n, lane == FEAT_POS_LO + n)
        slope_feat = jnp.where(hit, term, slope_feat)
    q_slc = jnp.concatenate([qr2, jnp.where(lane < NSA_MAX_BLOCKS, sel_bias, slope_feat).astype(BF16)], axis=1)
    q_win = jnp.concatenate([qr2, slope_feat.astype(BF16)], axis=1)

    col = lax.broadcasted_iota(jnp.int32, (1, rows), 1)
    t_col = (t0 + col - (col // tq) * tq).astype(F32)
    key_sub = lax.broadcasted_iota(jnp.int32, (kchunk, 1), 0)

    def slc_chunk(c, carry, causal):
        m, l, acc = carry
        k0 = pl.multiple_of(c * kchunk, kchunk)
        kb = jnp.concatenate([ks_ref[pl.ds(k0, kchunk), :].astype(BF16), feat_ref[pl.ds(k0, kchunk), :]], axis=1)
        s = _dot_nt(kb, q_slc)
        if causal:
            s = jnp.where((k0 + key_sub).astype(F32) <= t_col, s, -MASK_BIG)
        m_new = jnp.maximum(m, jnp.max(s, axis=0, keepdims=True))
        alpha = jnp.exp2(m - m_new)
        p = jnp.exp2(s - m_new)
        l = alpha * l + jnp.sum(p, axis=0, keepdims=True)
        v_t = vs_ref[pl.ds(k0, kchunk), :].T.astype(BF16)
        acc = alpha * acc + _dot(v_t, p.astype(BF16))
        return m_new, l, acc

    n_full = t0 // kchunk
    carry = lax.fori_loop(
        0, n_full, lambda c, cr: slc_chunk(c, cr, False),
        (jnp.full((1, rows), NEG_INF, F32), jnp.zeros((1, rows), F32), jnp.zeros((HEAD_DIM, rows), F32)))
    _, l_s, acc_s = slc_chunk(n_full, carry, True)
    o_slc = (acc_s / l_s).T

    wk = WINDOW + tq

    def window_attend(w0, mask_fn):
        kb = jnp.concatenate([kw_ref[pl.ds(w0, wk), :].astype(BF16), feat_ref[pl.ds(w0, wk), :]], axis=1)
        pos = (w0 + lax.broadcasted_iota(jnp.int32, (wk, 1), 0)).astype(F32)
        s = mask_fn(_dot_nt(kb, q_win), pos)
        p = jnp.exp2(s - jnp.max(s, axis=0, keepdims=True))
        v_t = vw_ref[pl.ds(w0, wk), :].T.astype(BF16)
        o_t = _dot(v_t, p.astype(BF16)) / jnp.sum(p, axis=0, keepdims=True)
        return o_t.T

    def banded(s, pos):
        left = jnp.where(pos[:tq] > t_col - float(WINDOW), s[:tq], -MASK_BIG)
        right = jnp.where(pos[wk - tq:] <= t_col, s[wk - tq:], -MASK_BIG)
        return jnp.concatenate([left, s[tq:wk - tq], right], axis=0)

    def head_of_sequence(s, pos):
        return jnp.where(pos <= t_col, s, -MASK_BIG)

    o_win = lax.cond(t0 >= WINDOW,
                     lambda: window_attend(pl.multiple_of(t0 - WINDOW, tq), banded),
                     lambda: window_attend(0, head_of_sequence))

    gt = _sigmoid(sm_ref[...])
    lane_s = lax.broadcasted_iota(jnp.int32, (1, SMALL_W), 1)
    branches = (o_cmp, o_slc, o_win)
    for h in range(HPG):
        out_h = jnp.zeros((tq, HEAD_DIM), F32)
        for br in range(3):
            col = SM_NG + br * NSA_HEADS + g * HPG + h
            gate = jnp.sum(jnp.where(lane_s == col, gt, 0.0), axis=-1, keepdims=True)
            out_h = out_h + gate * branches[br][h * tq:(h + 1) * tq]
        o_ref[:, h * HEAD_DIM:(h + 1) * HEAD_DIM] = out_h.astype(o_ref.dtype)


def _nsa_attend(proj, small, cmp_kv, c2s, feats, batch, seq):
    t = proj.shape[0]
    tq = min(NSA_TQ, seq)
    kchunk = min(NSA_KC, seq)
    nq = seq // tq
    ns = cmp_kv.shape[-2]
    assert seq // SLC_BLOCK <= NSA_MAX_BLOCKS and WINDOW % tq == 0 and kchunk % tq == 0 and WINDOW + tq <= seq
    kv_blk = lambda idx: pl.BlockSpec(
        (seq, HEAD_DIM), lambda b, g, i: (b, (C_KV + idx * NSA_KW) // HEAD_DIM + g))
    cmp_blk = lambda kv: pl.BlockSpec((1, 1, 1, ns, HEAD_DIM), lambda b, g, i: (b, kv, g, 0, 0))
    kern = functools.partial(_nsa_kernel, tq=tq, seq=seq, kchunk=kchunk)
    return pl.pallas_call(
        kern,
        grid=(batch, NSA_GROUPS, nq),
        in_specs=[pl.BlockSpec((tq, NSA_KW), lambda b, g, i: (b * nq + i, C_NQ // NSA_KW + g)),
                  cmp_blk(0), cmp_blk(1),
                  kv_blk(2), kv_blk(3), kv_blk(4), kv_blk(5),
                  pl.BlockSpec((tq, SMALL_W), lambda b, g, i: (b * nq + i, 0)),
                  pl.BlockSpec((NSA_NBP, ns), lambda b, g, i: (0, 0)),
                  pl.BlockSpec((seq, HEAD_DIM), lambda b, g, i: (0, 0))],
        out_specs=pl.BlockSpec((tq, NSA_KW), lambda b, g, i: (b * nq + i, g)),
        out_shape=jax.ShapeDtypeStruct((t, NSA_QW), BF16),
        compiler_params=_cparams(("parallel", "parallel", "arbitrary")),
    )(proj, cmp_kv, cmp_kv, proj, proj, proj, proj, small, c2s, feats)


def _readout_kernel(oa_ref, ob_ref, wa_ref, wb_ref, ma_ref, mb_ref, y_ref):
    ya = _dot(oa_ref[...], wa_ref[...])
    yb = _dot(ob_ref[...], wb_ref[...])
    y_ref[...] = (_sigmoid(ma_ref[...]) * ya + _sigmoid(mb_ref[...]) * yb).astype(y_ref.dtype)


def _readout(o_a, o_b, w_a, w_b, layer, proj, d_model, tm, tn):
    t = o_a.shape[0]
    tm, tn = min(tm, t), min(tn, d_model)
    c_mb = C_MA + d_model
    return pl.pallas_call(
        _readout_kernel,
        grid=(t // tm, d_model // tn),
        in_specs=[pl.BlockSpec((tm, GDN_W), lambda i, j: (i, 0)),
                  pl.BlockSpec((tm, NSA_QW), lambda i, j: (i, 0)),
                  pl.BlockSpec((None, GDN_W, tn), lambda i, j: (layer, 0, j)),
                  pl.BlockSpec((None, NSA_QW, tn), lambda i, j: (layer, 0, j)),
                  pl.BlockSpec((tm, tn), lambda i, j: (i, C_MA // tn + j)),
                  pl.BlockSpec((tm, tn), lambda i, j: (i, c_mb // tn + j))],
        out_specs=pl.BlockSpec((tm, tn), lambda i, j: (i, j)),
        out_shape=jax.ShapeDtypeStruct((t, d_model), BF16),
        compiler_params=_cparams(("parallel", "parallel")),
    )(o_a, o_b, w_a, w_b, proj, proj)


def _cmp_to_slc(seq):
    ns = seq // CMP_STRIDE
    nb = seq // SLC_BLOCK
    start = np.arange(ns) * CMP_STRIDE
    end = start + CMP_BLOCK - 1
    s0 = np.arange(nb) * SLC_BLOCK
    m = (end[:, None] >= s0[None, :]) & (start[:, None] <= s0[None, :] + SLC_BLOCK - 1)
    m[ns - 1] = False
    out = np.zeros((NSA_NBP, ns), np.float32)
    out[:nb] = m.T
    return jnp.asarray(out, dtype=BF16)


PERM_TR = 1024
PERM_TAIL = 128


def _permute_kernel(a_ref, b_ref, o_ref, *, ranges):
    j = pl.program_id(1)
    x = jnp.concatenate([a_ref[...], b_ref[...]], axis=1)
    for lo, hi, shift in ranges:
        @pl.when(jnp.logical_and(j >= lo, j < hi))
        def _():
            o_ref[...] = x[:, shift:shift + PROJ_TN].astype(o_ref.dtype)


def _permute_w_in(w_in, d_model):
    depth, d, _ = w_in.shape
    g0 = 4 * GDN_W
    s0 = g0 + 2 * GDN_HEADS
    s1 = s0 + NSA_QW + 6 * NSA_KW
    s2 = s1 + 3 * NSA_HEADS
    n_main = C_MA + 2 * d_model
    bounds = (0, g0 // PROJ_TN, C_MA // PROJ_TN, n_main // PROJ_TN)
    shifts = (0, s0 - g0, s2 - C_MA)
    assert g0 % PROJ_TN == 0 and C_MA % PROJ_TN == 0 and n_main % PROJ_TN == 0 and max(shifts) <= PERM_TAIL
    ranges = tuple((bounds[r], bounds[r + 1], shifts[r]) for r in range(3))
    tr = min(PERM_TR, d)
    tail_per_blk = PROJ_TN // PERM_TAIL
    w_bf = w_in.astype(BF16)
    main = pl.pallas_call(
        functools.partial(_permute_kernel, ranges=ranges),
        grid=(depth, n_main // PROJ_TN, d // tr),
        in_specs=[pl.BlockSpec((None, tr, PROJ_TN), lambda l, j, r: (l, r, j)),
                  pl.BlockSpec((None, tr, PERM_TAIL), lambda l, j, r: (l, r, (j + 1) * tail_per_blk))],
        out_specs=pl.BlockSpec((None, tr, PROJ_TN), lambda l, j, r: (l, r, j)),
        out_shape=jax.ShapeDtypeStruct((depth, d, n_main), BF16),
        compiler_params=_cparams(("parallel", "parallel", "parallel")),
    )(w_bf, w_bf)
    small = jnp.concatenate([w_bf[..., g0:s0], w_bf[..., s1:s2],
                             jnp.zeros((depth, d, SMALL_W - (s0 - g0) - (s2 - s1)), BF16)], axis=-1)
    return main, small


def _mixer(u, layer, proj_w, small_w, conv_w, a_log, dt_bias, norm_w, cmp_pos, cmp_w1, cmp_w2, w_read_a, w_read_b,
           w_out, c2s, feats, batch, seq, d_model):
    n_main = proj_w.shape[-1]
    proj = _matmul(u, proj_w, layer, tm=1024, tn=2 * PROJ_TN if n_main % (2 * PROJ_TN) == 0 else PROJ_TN)
    small = _matmul(u, small_w, layer, tm=1024, tn=SMALL_W)

    qkv = _gdn_prep(proj, conv_w, seq, ts=512)
    bcast = lambda v: jnp.broadcast_to(v.reshape(GDN_HEADS, 1, 1), (GDN_HEADS, 1, HEAD_DIM))
    o_a = _gdn_chunk(qkv, proj, small, bcast(a_log), bcast(dt_bias), norm_w.reshape(1, HEAD_DIM), batch, seq)

    w1cat = jnp.concatenate([cmp_w1[:, :CMP_STRIDE], cmp_w1[:, CMP_STRIDE:]], axis=-1).astype(BF16)
    cmp_kv = _nsa_compress(proj, w1cat, cmp_pos, cmp_w2.astype(BF16), batch, seq)
    o_b = _nsa_attend(proj, small, cmp_kv, c2s, feats, batch, seq)

    y = _readout(o_a, o_b, w_read_a, w_read_b, layer, proj, d_model, tm=1024, tn=512)
    return _matmul(y, w_out, layer, tm=1024, tn=1024)


def kernel(x, c, ada_w, ada_b, ada_table, w_in, gdn_conv_w, gdn_a_log, gdn_dt_bias, gdn_norm_w,
           cmp_pos, cmp_w1, cmp_w2, w_read_a, w_read_b, w_out, ln1_g, ln1_b,
           mlp_w1, mlp_b1, mlp_w2, mlp_b2, ln2_g, ln2_b):
    batch, seq, d_model = x.shape
    depth = w_in.shape[0]
    d_ff = mlp_w1.shape[-1]
    t = batch * seq
    alpha = (2.0 * depth) ** 0.25
    c2s = _cmp_to_slc(seq)
    feats = _key_features(seq)

    proj_w, small_w = _permute_w_in(w_in, d_model)
    w_read_a, w_read_b, w_out = (w.astype(BF16) for w in (w_read_a, w_read_b, w_out))
    mlp_w1, mlp_w2 = mlp_w1.astype(BF16), mlp_w2.astype(BF16)

    mod = _ada_projection(c, ada_w, ada_b)

    x2 = x.reshape(t, d_model)
    u = None
    for l in range(depth):
        m = (mod + ada_table[l].reshape(1, -1)).reshape(batch, N_ADA, 1, d_model)
        shift1, scale1, gate1, shift2, scale2, gate2 = (m[:, j] for j in range(N_ADA))
        if u is None:
            u = _modulate(x2, scale1, shift1, seq, ts=512)
        h = _mixer(u, l, proj_w, small_w, gdn_conv_w[l], gdn_a_log[l], gdn_dt_bias[l],
                   gdn_norm_w[l], cmp_pos[l], cmp_w1[l], cmp_w2[l], w_read_a, w_read_b, w_out,
                   c2s, feats, batch, seq, d_model)
        x2, u = _deepnorm_ln(x2, h, gate1, ln1_g[l], ln1_b[l], seq, 256, alpha, scale2, shift2)

        tn1 = min(1024, d_ff)
        act = _matmul(u, mlp_w1, l, tm=1024, tn=tn1, out_dtype=BF16,
                      extras=(mlp_b1[l].reshape(1, d_ff),),
                      extra_specs=(pl.BlockSpec((1, tn1), lambda i, j, k: (0, j)),),
                      epilogue=lambda acc, b: jnp.square(jnp.maximum(acc + b, 0.0)))
        tn2 = min(1024, d_model)
        h = _matmul(act, mlp_w2, l, tm=1024, tn=tn2, tk=4096,
                    extras=(mlp_b2[l].reshape(1, d_model),),
                    extra_specs=(pl.BlockSpec((1, tn2), lambda i, j, k: (0, j)),),
                    epilogue=lambda acc, b: acc + b)
        if l + 1 < depth:
            m_next = (mod + ada_table[l + 1].reshape(1, -1)).reshape(batch, N_ADA, 1, d_model)
            x2, u = _deepnorm_ln(x2, h, gate2, ln2_g[l], ln2_b[l], seq, 256, alpha,
                                 m_next[:, 1], m_next[:, 0])
        else:
            x2, _ = _deepnorm_ln(x2, h, gate2, ln2_g[l], ln2_b[l], seq, 256, alpha)
    return x2.reshape(batch, seq, d_model)
```
